```python
import math
import jax, jax.numpy as jnp
from jax import lax
import numpy as np

D_MODEL = 2048
BATCH = 4
SEQ = 4096
DEPTH = 4

MLSTM_HEADS = 4
MLSTM_WIDTH = D_MODEL
MLSTM_DV = MLSTM_WIDTH // MLSTM_HEADS
MLSTM_DK = MLSTM_DV // 2
MLSTM_QK = MLSTM_HEADS * MLSTM_DK
MLSTM_CHUNK = 64
SSD_WIDTH = D_MODEL
SSD_HEADDIM = 64
SSD_HEADS = SSD_WIDTH // SSD_HEADDIM
SSD_GROUPS = 8
SSD_STATE = 128
SSD_GN = SSD_GROUPS * SSD_STATE
SSD_CONV_DIM = SSD_WIDTH + 2 * SSD_GN
SSD_CHUNK = 128
CONV_K = 4
FFN_DENSE = 5632
N_EXPERTS = 8
TOP_K = 2
FFN_EXPERT = 2 * D_MODEL
N_DENSE_LAYERS = (DEPTH + 1) // 2
N_MOE_LAYERS = DEPTH // 2
EPS = 1e-6
IN_SIZES = (MLSTM_QK, MLSTM_QK, MLSTM_WIDTH, MLSTM_WIDTH, MLSTM_HEADS, MLSTM_HEADS,
            SSD_WIDTH, SSD_CONV_DIM, SSD_HEADS, 2 * D_MODEL)
IN_SPLITS = tuple(sum(IN_SIZES[:i + 1]) for i in range(len(IN_SIZES) - 1))
IN_TOTAL = sum(IN_SIZES)

kernel_name = 'hybrid_mlstm_ssd_gated_moe_trunk'


def rmsnorm(x, g):
    xf = x.astype(jnp.float32)
    xf = xf * lax.rsqrt(jnp.mean(xf * xf, axis=-1, keepdims=True) + EPS)
    return xf.astype(x.dtype) * g


def causal_dwconv(x, w, b):
    s = x.shape[1]
    xp = jnp.pad(x, ((0, 0), (CONV_K - 1, 0), (0, 0)))
    y = b
    for k in range(CONV_K):
        y = y + xp[:, k:k + s] * w[k]
    return y


def swiglu(x, w_gu, w_down):
    gate, up = jnp.split(x @ w_gu, 2, axis=-1)
    return (jax.nn.silu(gate) * up) @ w_down


def segsum(a):
    t = a.shape[-1]
    ar = jnp.broadcast_to(a[..., :, None], a.shape + (t,))
    strict = jnp.tril(jnp.ones((t, t), bool), -1)
    cs = jnp.cumsum(jnp.where(strict, ar, 0.0), axis=-2)
    return jnp.where(jnp.tril(jnp.ones((t, t), bool)), cs, -jnp.inf)


def mlstm_chunkwise(q, k, v, i_pre, f_pre):
    bsz, nh, s, dk = q.shape
    dv = v.shape[-1]
    nc = s // MLSTM_CHUNK
    q = q * (dk ** -0.5)
    log_f = jax.nn.log_sigmoid(f_pre)

    def to_chunks(a):
        a = a.reshape((bsz, nh, nc, MLSTM_CHUNK) + a.shape[3:])
        return jnp.moveaxis(a, 2, 0)

    causal = jnp.tril(jnp.ones((MLSTM_CHUNK, MLSTM_CHUNK), bool))

    def step(carry, inp):
        c_st, n_st, m_st = carry
        qc, kc, vc, ic, lfc = inp
        b = jnp.cumsum(lfc, axis=-1)
        log_d = jnp.where(causal, b[..., :, None] - b[..., None, :] + ic[..., None, :], -jnp.inf)
        m_inter = b + m_st[..., None]
        m_t = jnp.maximum(m_inter, jnp.max(log_d, axis=-1))
        scores = jnp.einsum('bhld,bhsd->bhls', qc, kc) * jnp.exp(log_d - m_t[..., None])
        inter = jnp.exp(m_inter - m_t)
        num = (jnp.einsum('bhls,bhsv->bhlv', scores, vc)
               + inter[..., None] * jnp.einsum('bhld,bhvd->bhlv', qc, c_st))
        den = jnp.sum(scores, axis=-1) + inter * jnp.einsum('bhld,bhd->bhl', qc, n_st)
        h = num / jnp.maximum(jnp.abs(den), jnp.exp(-m_t))[..., None]
        b_last = b[..., -1]
        log_w = b_last[..., None] - b + ic
        m_new = jnp.maximum(b_last + m_st, jnp.max(log_w, axis=-1))
        w = jnp.exp(log_w - m_new[..., None])
        decay = jnp.exp(b_last + m_st - m_new)
        c_new = decay[..., None, None] * c_st + jnp.einsum('bhsv,bhsd->bhvd', vc * w[..., None], kc)
        n_new = decay[..., None] * n_st + jnp.einsum('bhs,bhsd->bhd', w, kc)
        return (c_new, n_new, m_new), h

    init = (jnp.zeros((bsz, nh, dv, dk), jnp.float32),
            jnp.zeros((bsz, nh, dk), jnp.float32),
            jnp.zeros((bsz, nh), jnp.float32))
    _, h = lax.scan(step, init, (to_chunks(q), to_chunks(k), to_chunks(v),
                                 to_chunks(i_pre), to_chunks(log_f)))
    return jnp.moveaxis(h, 0, 2).reshape(bsz, nh, s, dv)


def mlstm_branch(q_raw, k_raw, v, o_pre, i_pre, f_pre, if_bias, conv_w, conv_b, norm_g):
    bsz, s, _ = v.shape
    qk = jax.nn.silu(causal_dwconv(jnp.concatenate([q_raw, k_raw], axis=-1), conv_w, conv_b))
    q, k = jnp.split(qk, 2, axis=-1)

    def heads(t, d):
        return t.reshape(bsz, s, MLSTM_HEADS, d).transpose(0, 2, 1, 3).astype(jnp.float32)

    ig = (i_pre + if_bias[:MLSTM_HEADS]).astype(jnp.float32).transpose(0, 2, 1)
    fg = (f_pre + if_bias[MLSTM_HEADS:]).astype(jnp.float32).transpose(0, 2, 1)
    h = mlstm_chunkwise(heads(q, MLSTM_DK), heads(k, MLSTM_DK), heads(v, MLSTM_DV), ig, fg)
    h = h * lax.rsqrt(jnp.mean(h * h, axis=-1, keepdims=True) + EPS)
    h = h.transpose(0, 2, 1, 3).reshape(bsz, s, MLSTM_WIDTH).astype(v.dtype) * norm_g
    return h * jax.nn.sigmoid(o_pre)


def ssd_chunked(x, dt, a, bm, cm):
    bsz, s, nh, p = x.shape
    g, n = bm.shape[2:]
    r = nh // g
    nc = s // SSD_CHUNK
    xdt = (x * dt[..., None]).reshape(bsz, nc, SSD_CHUNK, g, r, p)
    a_cl = jnp.moveaxis((dt * a).reshape(bsz, nc, SSD_CHUNK, g, r), 2, -1)
    bc = bm.reshape(bsz, nc, SSD_CHUNK, g, n)
    cc = cm.reshape(bsz, nc, SSD_CHUNK, g, n)
    a_cum = jnp.cumsum(a_cl, axis=-1)
    cb = jnp.einsum('bclgn,bcsgn->bcgls', cc, bc)
    att = cb[:, :, :, None] * jnp.exp(segsum(a_cl))
    y_diag = jnp.einsum('bcgrls,bcsgrp->bclgrp', att, xdt)
    decay_to_end = jnp.moveaxis(jnp.exp(a_cum[..., -1:] - a_cum), -1, 2)
    states = jnp.einsum('bcsgn,bcsgrp->bcgrpn', bc, xdt * decay_to_end[..., None])
    a_last = jnp.moveaxis(a_cum[..., -1], 1, -1)
    chunk_decay = jnp.exp(segsum(jnp.pad(a_last, ((0, 0), (0, 0), (0, 0), (1, 0)))))
    states = jnp.concatenate([jnp.zeros_like(states[:, :1]), states], axis=1)
    prev = jnp.einsum('bgrzc,bcgrpn->bzgrpn', chunk_decay, states)[:, :-1]
    decay_from_start = jnp.moveaxis(jnp.exp(a_cum), -1, 2)
    y_off = jnp.einsum('bclgn,bcgrpn->bclgrp', cc, prev) * decay_from_start[..., None]
    return (y_diag + y_off).reshape(bsz, s, nh, p)


def ssd_branch(z, xbc, dt_raw, conv_w, conv_b, dt_bias, a_log, d_skip, norm_g):
    bsz, s, _ = z.shape
    xbc = jax.nn.silu(causal_dwconv(xbc, conv_w, conv_b))
    xs, bm, cm = jnp.split(xbc, [SSD_WIDTH, SSD_WIDTH + SSD_GN], axis=-1)
    xs = xs.reshape(bsz, s, SSD_HEADS, SSD_HEADDIM).astype(jnp.float32)
    bm = bm.reshape(bsz, s, SSD_GROUPS, SSD_STATE).astype(jnp.float32)
    cm = cm.reshape(bsz, s, SSD_GROUPS, SSD_STATE).astype(jnp.float32)
    dt = jax.nn.softplus(dt_raw.astype(jnp.float32) + dt_bias.astype(jnp.float32))
    a = -jnp.exp(a_log.astype(jnp.float32))
    y = ssd_chunked(xs, dt, a, bm, cm) + d_skip.astype(jnp.float32)[:, None] * xs
    y = y.reshape(bsz, s, SSD_WIDTH) * jax.nn.silu(z.astype(jnp.float32))
    yg = y.reshape(bsz, s, SSD_GROUPS, SSD_WIDTH // SSD_GROUPS)
    yg = yg * lax.rsqrt(jnp.mean(yg * yg, axis=-1, keepdims=True) + EPS)
    return yg.reshape(bsz, s, SSD_WIDTH).astype(z.dtype) * norm_g


def hybrid_mixer(h, w_in, if_bias, qk_conv_w, qk_conv_b, mh_norm_g, ssd_conv_w, ssd_conv_b,
                 dt_bias, a_log, d_skip, ssd_norm_g, gate_bias, w_branch_a, w_branch_b, w_out):
    proj = h @ w_in
    q_raw, k_raw, v, o_pre, i_pre, f_pre, z, xbc, dt_raw, gate_pre = jnp.split(proj, IN_SPLITS, axis=-1)
    y_a = mlstm_branch(q_raw, k_raw, v, o_pre, i_pre, f_pre, if_bias, qk_conv_w, qk_conv_b, mh_norm_g)
    y_b = ssd_branch(z, xbc, dt_raw, ssd_conv_w, ssd_conv_b, dt_bias, a_log, d_skip, ssd_norm_g)
    gates = jax.nn.sigmoid(gate_pre + gate_bias)
    g_a, g_b = jnp.split(gates, 2, axis=-1)
    merged = g_a * (y_a @ w_branch_a) + g_b * (y_b @ w_branch_b)
    return merged @ w_out


def moe_swiglu(h, w_router, b_router, w_gu, w_down):
    tok = h.reshape(-1, h.shape[-1])
    logits = (tok @ w_router).astype(jnp.float32) + b_router.astype(jnp.float32)
    top_logit, top_idx = lax.top_k(logits, TOP_K)
    top_w = jax.nn.softmax(top_logit, axis=-1)
    y = jnp.zeros_like(tok)
    for e in range(N_EXPERTS):
        w_e = jnp.sum(jnp.where(top_idx == e, top_w, 0.0), axis=-1).astype(tok.dtype)
        y = y + w_e[:, None] * swiglu(tok, w_gu[e], w_down[e])
    return y.reshape(h.shape)


def _normal(key, shape, scale):
    return scale * jax.random.normal(key, shape, jnp.float32)


def setup_inputs(seed: int = 0) -> dict:
    key = jax.random.key(seed)
    ks = jax.random.split(key, 32)
    d = D_MODEL
    dt0 = jnp.exp(jax.random.uniform(ks[9], (DEPTH, SSD_HEADS), jnp.float32)
                  * (math.log(0.1) - math.log(0.001)) + math.log(0.001))
    if_bias = jnp.concatenate([
        _normal(ks[3], (DEPTH, MLSTM_HEADS), 0.1),
        3.0 + 3.0 * jax.random.uniform(ks[4], (DEPTH, MLSTM_HEADS), jnp.float32)], axis=-1)
    return {
        'x': jax.random.normal(ks[0], (BATCH, SEQ, d), jnp.float32),
        'norm_mix_g': 1.0 + _normal(ks[1], (DEPTH, d), 0.02),
        'w_in': _normal(ks[2], (DEPTH, d, IN_TOTAL), d ** -0.5),
        'if_bias': if_bias,
        'qk_conv_w': _normal(ks[5], (DEPTH, CONV_K, 2 * MLSTM_QK), CONV_K ** -0.5),
        'qk_conv_b': _normal(ks[6], (DEPTH, 2 * MLSTM_QK), 0.01),
        'mh_norm_g': 1.0 + _normal(ks[7], (DEPTH, MLSTM_WIDTH), 0.02),
        'ssd_conv_w': _normal(ks[8], (DEPTH, CONV_K, SSD_CONV_DIM), CONV_K ** -0.5),
        'ssd_conv_b': _normal(ks[10], (DEPTH, SSD_CONV_DIM), 0.01),
        'dt_bias': dt0 + jnp.log(-jnp.expm1(-dt0)),
        'a_log': jnp.log(jax.random.uniform(ks[11], (DEPTH, SSD_HEADS), jnp.float32, 1.0, 16.0)),
        'd_skip': 1.0 + _normal(ks[12], (DEPTH, SSD_HEADS), 0.1),
        'ssd_norm_g': 1.0 + _normal(ks[13], (DEPTH, SSD_WIDTH), 0.02),
        'gate_bias': _normal(ks[14], (DEPTH, 2 * d), 0.01),
        'w_branch_a': _normal(ks[15], (DEPTH, MLSTM_WIDTH, d), MLSTM_WIDTH ** -0.5),
        'w_branch_b': _normal(ks[16], (DEPTH, SSD_WIDTH, d), SSD_WIDTH ** -0.5),
        'w_out': _normal(ks[17], (DEPTH, d, d), d ** -0.5),
        'norm_ffn_g': 1.0 + _normal(ks[18], (DEPTH, d), 0.02),
        'ffn_w_gu': _normal(ks[19], (N_DENSE_LAYERS, d, 2 * FFN_DENSE), d ** -0.5),
        'ffn_w_down': _normal(ks[20], (N_DENSE_LAYERS, FFN_DENSE, d), FFN_DENSE ** -0.5),
        'router_w': _normal(ks[21], (N_MOE_LAYERS, d, N_EXPERTS), d ** -0.5),
        'router_b': _normal(ks[22], (N_MOE_LAYERS, N_EXPERTS), 0.01),
        'exp_w_gu': _normal(ks[23], (N_MOE_LAYERS, N_EXPERTS, d, 2 * FFN_EXPERT), d ** -0.5),
        'exp_w_down': _normal(ks[24], (N_MOE_LAYERS, N_EXPERTS, FFN_EXPERT, d), FFN_EXPERT ** -0.5),
        'norm_final_g': 1.0 + _normal(ks[25], (d,), 0.02),
    }


def reference(x, norm_mix_g, w_in, if_bias, qk_conv_w, qk_conv_b, mh_norm_g, ssd_conv_w,
              ssd_conv_b, dt_bias, a_log, d_skip, ssd_norm_g, gate_bias, w_branch_a, w_branch_b,
              w_out, norm_ffn_g, ffn_w_gu, ffn_w_down, router_w, router_b, exp_w_gu, exp_w_down,
              norm_final_g):
    for l in range(DEPTH):
        h = rmsnorm(x, norm_mix_g[l])
        x = x + hybrid_mixer(h, w_in[l], if_bias[l], qk_conv_w[l], qk_conv_b[l], mh_norm_g[l],
                             ssd_conv_w[l], ssd_conv_b[l], dt_bias[l], a_log[l], d_skip[l],
                             ssd_norm_g[l], gate_bias[l], w_branch_a[l], w_branch_b[l], w_out[l])
        h = rmsnorm(x, norm_ffn_g[l])
        if l % 2 == 0:
            x = x + swiglu(h, ffn_w_gu[l // 2], ffn_w_down[l // 2])
        else:
            x = x + moe_swiglu(h, router_w[l // 2], router_b[l // 2], exp_w_gu[l // 2], exp_w_down[l // 2])
    return rmsnorm(x, norm_final_g)
```

```python
import functools
import math

import jax
import jax.numpy as jnp
from jax import lax
from jax.experimental import pallas as pl
from jax.experimental.pallas import tpu as pltpu

F32 = jnp.float32
BF16 = jnp.bfloat16

D_MODEL = 2048
DEPTH = 4
EPS = 1e-6
CONV_K = 4
ML_HEADS = 4
ML_DV = 512
ML_DK = 256
ML_QK = ML_HEADS * ML_DK
ML_WIDTH = ML_HEADS * ML_DV
SSD_HEADS = 32
SSD_P = 64
SSD_GROUPS = 8
SSD_N = 128
SSD_R = SSD_HEADS // SSD_GROUPS
SSD_WIDTH = SSD_HEADS * SSD_P
SSD_GN = SSD_GROUPS * SSD_N
SSD_CONV = SSD_WIDTH + 2 * SSD_GN
SSD_GW = SSD_R * SSD_P
FFN_DENSE = 5632
N_EXPERTS = 8
TOP_K = 2
FFN_EXPERT = 4096

PC_Q = 0
PC_K = 1024
PC_V = 2048
PC_O = 4096
PC_Z = 6144
PC_XBC = 8192
PC_GATE = 12288
PC_TOTAL = 16384
SM_W = 128
SM_DT = 32

V7X_VMEM_BYTES = 64 * 1024 * 1024
VMEM_LIMIT = 56 * 1024 * 1024

SEQ_CHUNK = 256
CARRY = 8


def _cparams(*sem):
    return pltpu.CompilerParams(dimension_semantics=sem, vmem_limit_bytes=VMEM_LIMIT)


def _sigmoid(x):
    return 1.0 / (1.0 + jnp.exp(-x))


def _silu(x):
    return x * _sigmoid(x)


def _softplus(x):
    return jnp.maximum(x, 0.0) + jnp.log(1.0 + jnp.exp(-jnp.abs(x)))


def _log_sigmoid(x):
    return jnp.minimum(x, 0.0) - jnp.log(1.0 + jnp.exp(-jnp.abs(x)))


def _split3(x):
    x1 = x.astype(BF16)
    r1 = x - x1.astype(F32)
    x2 = r1.astype(BF16)
    x3 = (r1 - x2.astype(F32)).astype(BF16)
    return x1, x2, x3


def _dot(a, b):
    return jnp.dot(a, b, preferred_element_type=F32)


def _dot_nt(a, b):
    return lax.dot_general(a, b, (((1,), (1,)), ((), ())), preferred_element_type=F32)


def _dot_tn(a, b):
    return lax.dot_general(a, b, (((0,), (0,)), ((), ())), preferred_element_type=F32)


def _dot_f32_left(x, m01):
    x1, x2, x3 = _split3(x)
    return _dot(x1, m01) + _dot(x2, m01) + _dot(x3, m01)


def _dot_f32_right(m01, x):
    x1, x2, x3 = _split3(x)
    return _dot(m01, x1) + _dot(m01, x2) + _dot(m01, x3)


def _rmsnorm_rows(x, g):
    ms = jnp.mean(x * x, axis=-1, keepdims=True)
    return (x * lax.rsqrt(ms + EPS)) * g


def _proj_kernel(x_ref, g_ref, w_ref, ws_ref, wst_ref, o_ref, os_ref, ost_ref, h_ref):
    @pl.when(pl.program_id(1) == 0)
    def _():
        hb = _rmsnorm_rows(x_ref[...], g_ref[...]).astype(BF16)
        h_ref[...] = hb
        os_ref[...] = _dot(hb, ws_ref[...])
        ost_ref[...] = _dot_nt(wst_ref[...], hb)

    o_ref[...] = _dot(h_ref[...], w_ref[...]).astype(o_ref.dtype)


def _proj(x, g, w_main, w_small, w_small_t, *, tm=1024, tn=1024):
    t, d = x.shape
    n = w_main.shape[1]
    return pl.pallas_call(
        _proj_kernel,
        grid=(t // tm, n // tn),
        in_specs=[
            pl.BlockSpec((tm, d), lambda i, j: (i, 0)),
            pl.BlockSpec((1, d), lambda i, j: (0, 0)),
            pl.BlockSpec((d, tn), lambda i, j: (0, j)),
            pl.BlockSpec((d, SM_W), lambda i, j: (0, 0)),
            pl.BlockSpec((SM_W, d), lambda i, j: (0, 0)),
        ],
        out_specs=[
            pl.BlockSpec((tm, tn), lambda i, j: (i, j)),
            pl.BlockSpec((tm, SM_W), lambda i, j: (i, 0)),
            pl.BlockSpec((SM_W, tm), lambda i, j: (0, i)),
        ],
        out_shape=[
            jax.ShapeDtypeStruct((t, n), BF16),
            jax.ShapeDtypeStruct((t, SM_W), F32),
            jax.ShapeDtypeStruct((SM_W, t), F32),
        ],
        scratch_shapes=[pltpu.VMEM((tm, d), BF16)],
        compiler_params=_cparams("parallel", "arbitrary"),
        name="proj",
    )(x, g, w_main, w_small, w_small_t)


def _causal_conv(raw, xc_ref, carry_ref, w_ref, b_ref, first_chunk):
    length = raw.shape[0]

    @pl.when(first_chunk)
    def _():
        carry_ref[...] = jnp.zeros_like(carry_ref)

    xc_ref[0:CARRY, :] = carry_ref[...]
    xc_ref[CARRY:CARRY + length, :] = raw
    carry_ref[...] = raw[length - CARRY:, :]
    y = b_ref[...] + w_ref[CONV_K - 1:CONV_K, :] * raw
    for back in range(1, CONV_K):
        tap = xc_ref[CARRY - back:CARRY - back + length, :]
        y = y + w_ref[CONV_K - 1 - back:CONV_K - back, :] * tap
    return y


def _mlstm_kernel(q_ref, k_ref, v_ref, o_ref, sm_ref, smt_ref, cw_ref, cb_ref, ifr_ref, ifc_ref,
                  g_ref, tril_ref, triu_ref, y_ref, xc_ref, carry_ref, c_ref, n_ref, m_ref):
    length = q_ref.shape[0]
    first = pl.program_id(1) == 0

    @pl.when(first)
    def _():
        c_ref[...] = jnp.zeros_like(c_ref)
        n_ref[...] = jnp.zeros_like(n_ref)
        m_ref[...] = jnp.zeros_like(m_ref)

    raw = jnp.concatenate([q_ref[...], k_ref[...]], axis=-1).astype(F32)
    qk = _silu(_causal_conv(raw, xc_ref, carry_ref, cw_ref, cb_ref, first))

    cols = sm_ref[...] + ifr_ref[...]
    rows = smt_ref[...] + ifc_ref[...]
    b_cols = _dot_f32_right(tril_ref[...], _log_sigmoid(cols))
    b_rows = _dot_f32_left(_log_sigmoid(rows), triu_ref[...])
    r_i = lax.broadcasted_iota(jnp.int32, (length, length), 0)
    c_i = lax.broadcasted_iota(jnp.int32, (length, length), 1)
    causal = r_i >= c_i

    for h in range(ML_HEADS):
        q_h = qk[:, h * ML_DK:(h + 1) * ML_DK] * (ML_DK ** -0.5)
        k_h = qk[:, ML_QK + h * ML_DK:ML_QK + (h + 1) * ML_DK]
        qb = q_h.astype(BF16)
        kb = k_h.astype(BF16)
        v_h = v_ref[:, h * ML_DV:(h + 1) * ML_DV]
        i_col = cols[:, h:h + 1]
        b_col = b_cols[:, ML_HEADS + h:ML_HEADS + h + 1]
        i_row = rows[h:h + 1, :]
        b_row = b_rows[ML_HEADS + h:ML_HEADS + h + 1, :]
        m_prev = m_ref[h:h + 1, 0:1]
        c_prev = c_ref[h]
        n_prev = n_ref[h:h + 1, :]

        log_d = jnp.where(causal, b_col - b_row + i_row, -jnp.inf)
        m_inter = b_col + m_prev
        m_t = jnp.maximum(m_inter, jnp.max(log_d, axis=-1, keepdims=True))
        scores = _dot_nt(qb, kb) * jnp.exp(log_d - m_t)
        inter = jnp.exp(m_inter - m_t)
        num = _dot(scores.astype(BF16), v_h) + inter * _dot(qb, c_prev.astype(BF16))
        den = (jnp.sum(scores, axis=-1, keepdims=True)
               + inter * jnp.sum(q_h * n_prev, axis=-1, keepdims=True))
        hh = num / jnp.maximum(jnp.abs(den), jnp.exp(-m_t))
        hh = hh * lax.rsqrt(jnp.mean(hh * hh, axis=-1, keepdims=True) + EPS)
        gate = _sigmoid(o_ref[:, h * ML_DV:(h + 1) * ML_DV].astype(F32))
        y_ref[:, h * ML_DV:(h + 1) * ML_DV] = (
            hh * g_ref[:, h * ML_DV:(h + 1) * ML_DV] * gate).astype(y_ref.dtype)

        b_last = b_col[length - 1:length, :]
        log_w_row = b_last - b_row + i_row
        m_new = jnp.maximum(b_last + m_prev, jnp.max(log_w_row, axis=-1, keepdims=True))
        w_col = jnp.exp(b_last - b_col + i_col - m_new)
        decay = jnp.exp(b_last + m_prev - m_new)
        vw = (v_h.astype(F32) * w_col).astype(BF16)
        c_ref[h] = decay * c_prev + _dot_tn(kb, vw)
        n_ref[h:h + 1, :] = decay * n_prev + jnp.sum(k_h * w_col, axis=0, keepdims=True)
        m_ref[h:h + 1, :] = jnp.broadcast_to(m_new, (1, m_ref.shape[1]))


def _mlstm(proj, small, small_t, conv_w, conv_b, if_row, if_col, norm_g, tril, triu, *, bsz, seq):
    length = SEQ_CHUNK
    nc = seq // length
    t = bsz * seq
    row = lambda b, c: b * nc + c
    return pl.pallas_call(
        _mlstm_kernel,
        grid=(bsz, nc),
        in_specs=[
            pl.BlockSpec((length, ML_QK), lambda b, c: (row(b, c), PC_Q // ML_QK)),
            pl.BlockSpec((length, ML_QK), lambda b, c: (row(b, c), PC_K // ML_QK)),
            pl.BlockSpec((length, ML_WIDTH), lambda b, c: (row(b, c), PC_V // ML_WIDTH)),
            pl.BlockSpec((length, ML_WIDTH), lambda b, c: (row(b, c), PC_O // ML_WIDTH)),
            pl.BlockSpec((length, SM_W), lambda b, c: (row(b, c), 0)),
            pl.BlockSpec((2 * ML_HEADS, length), lambda b, c: (0, row(b, c))),
            pl.BlockSpec((CONV_K, 2 * ML_QK), lambda b, c: (0, 0)),
            pl.BlockSpec((1, 2 * ML_QK), lambda b, c: (0, 0)),
            pl.BlockSpec((1, SM_W), lambda b, c: (0, 0)),
            pl.BlockSpec((2 * ML_HEADS, length), lambda b, c: (0, 0)),
            pl.BlockSpec((1, ML_WIDTH), lambda b, c: (0, 0)),
            pl.BlockSpec((length, length), lambda b, c: (0, 0)),
            pl.BlockSpec((length, length), lambda b, c: (0, 0)),
        ],
        out_specs=pl.BlockSpec((length, ML_WIDTH), lambda b, c: (row(b, c), 0)),
        out_shape=jax.ShapeDtypeStruct((t, ML_WIDTH), BF16),
        scratch_shapes=[
            pltpu.VMEM((CARRY + length, 2 * ML_QK), F32),
            pltpu.VMEM((CARRY, 2 * ML_QK), F32),
            pltpu.VMEM((ML_HEADS, ML_DK, ML_DV), F32),
            pltpu.VMEM((2 * ML_HEADS, ML_DK), F32),
            pltpu.VMEM((2 * ML_HEADS, 128), F32),
        ],
        compiler_params=_cparams("parallel", "arbitrary"),
        name="mlstm",
    )(proj, proj, proj, proj, small, small_t, conv_w, conv_b, if_row, if_col, norm_g, tril, triu)


def _ssd_kernel(xbc_ref, z_ref, sm_ref, smt_ref, cw_ref, cb_ref, dtb_r_ref, dtb_c_ref, al_r_ref,
                al_c_ref, dsk_ref, g_ref, tril_ref, triu_ref, e_ref, y_ref,
                xc_ref, carry_ref, st_ref):
    length = xbc_ref.shape[0]
    first = pl.program_id(1) == 0

    @pl.when(first)
    def _():
        st_ref[...] = jnp.zeros_like(st_ref)

    xconv = _silu(_causal_conv(xbc_ref[...].astype(F32), xc_ref, carry_ref, cw_ref, cb_ref, first))

    dt_c = _softplus(sm_ref[...] + dtb_r_ref[...])
    dt_r = _softplus(smt_ref[...] + dtb_c_ref[...])
    a_c = dt_c * (-jnp.exp(al_r_ref[...]))
    a_r = dt_r * (-jnp.exp(al_c_ref[...]))
    acum_c = _dot_f32_right(tril_ref[...], a_c)
    acum_r = _dot_f32_left(a_r, triu_ref[...])
    a_last = acum_c[length - 1:length, :]

    e01 = e_ref[...]
    dt_e = _dot_f32_left(dt_c, e01)
    from_start_e = _dot_f32_left(jnp.exp(acum_c), e01)
    to_end_e = _dot_f32_left(jnp.exp(a_last - acum_c), e01)
    chunk_e = _dot_f32_left(jnp.broadcast_to(jnp.exp(a_last), (8, SM_W)), e01)[0:1, :]

    r_i = lax.broadcasted_iota(jnp.int32, (length, length), 0)
    c_i = lax.broadcasted_iota(jnp.int32, (length, length), 1)
    causal = r_i >= c_i

    for g in range(SSD_GROUPS):
        ch = slice(g * SSD_GW, (g + 1) * SSD_GW)
        xs_g = xconv[:, ch]
        bm = xconv[:, SSD_WIDTH + g * SSD_N:SSD_WIDTH + (g + 1) * SSD_N].astype(BF16)
        cm = xconv[:, SSD_WIDTH + SSD_GN + g * SSD_N:SSD_WIDTH + SSD_GN + (g + 1) * SSD_N].astype(BF16)
        xdt = xs_g * dt_e[:, ch]
        cb = _dot_nt(cm, bm)
        prev = st_ref[g]
        y = _dot(cm, prev.astype(BF16)) * from_start_e[:, ch] + dsk_ref[:, ch] * xs_g
        diag = []
        for r in range(SSD_R):
            h = g * SSD_R + r
            seg = acum_c[:, SM_DT + h:SM_DT + h + 1] - acum_r[h:h + 1, :]
            att = cb * jnp.exp(jnp.where(causal, seg, -jnp.inf))
            diag.append(_dot(att.astype(BF16), xdt[:, r * SSD_P:(r + 1) * SSD_P].astype(BF16)))
        y = y + jnp.concatenate(diag, axis=-1)
        st_ref[g] = prev * chunk_e[:, ch] + _dot_tn(bm, (xdt * to_end_e[:, ch]).astype(BF16))
        y = y * _silu(z_ref[:, ch].astype(F32))
        y = y * lax.rsqrt(jnp.mean(y * y, axis=-1, keepdims=True) + EPS)
        y_ref[:, ch] = (y * g_ref[:, ch]).astype(y_ref.dtype)


def _ssd(proj, small, small_t, conv_w, conv_b, dtb_row, dtb_col, al_row, al_col, dskip_e, norm_g,
         tril, triu, expand, *, bsz, seq):
    length = SEQ_CHUNK
    nc = seq // length
    t = bsz * seq
    row = lambda b, c: b * nc + c
    const = lambda b, c: (0, 0)
    return pl.pallas_call(
        _ssd_kernel,
        grid=(bsz, nc),
        in_specs=[
            pl.BlockSpec((length, SSD_CONV), lambda b, c: (row(b, c), PC_XBC // SSD_CONV)),
            pl.BlockSpec((length, SSD_WIDTH), lambda b, c: (row(b, c), PC_Z // SSD_WIDTH)),
            pl.BlockSpec((length, SM_W), lambda b, c: (row(b, c), 0)),
            pl.BlockSpec((SSD_HEADS, length), lambda b, c: (SM_DT // SSD_HEADS, row(b, c))),
            pl.BlockSpec((CONV_K, SSD_CONV), const),
            pl.BlockSpec((1, SSD_CONV), const),
            pl.BlockSpec((1, SM_W), const),
            pl.BlockSpec((SSD_HEADS, length), const),
            pl.BlockSpec((1, SM_W), const),
            pl.BlockSpec((SSD_HEADS, length), const),
            pl.BlockSpec((1, SSD_WIDTH), const),
            pl.BlockSpec((1, SSD_WIDTH), const),
            pl.BlockSpec((length, length), const),
            pl.BlockSpec((length, length), const),
            pl.BlockSpec((SM_W, SSD_WIDTH), const),
        ],
        out_specs=pl.BlockSpec((length, SSD_WIDTH), lambda b, c: (row(b, c), 0)),
        out_shape=jax.ShapeDtypeStruct((t, SSD_WIDTH), BF16),
        scratch_shapes=[
            pltpu.VMEM((CARRY + length, SSD_CONV), F32),
            pltpu.VMEM((CARRY, SSD_CONV), F32),
            pltpu.VMEM((SSD_GROUPS, SSD_N, SSD_GW), F32),
        ],
        compiler_params=_cparams("parallel", "arbitrary"),
        name="ssd",
    )(proj, proj, small, small_t, conv_w, conv_b, dtb_row, dtb_col, al_row, al_col, dskip_e, norm_g,
      tril, triu, expand)


def _merge_kernel(ya_ref, yb_ref, wa_ref, wb_ref, ga_ref, gb_ref, ba_ref, bb_ref, o_ref):
    ga = _sigmoid(ga_ref[...].astype(F32) + ba_ref[...])
    gb = _sigmoid(gb_ref[...].astype(F32) + bb_ref[...])
    o_ref[...] = (ga * _dot(ya_ref[...], wa_ref[...]) + gb * _dot(yb_ref[...], wb_ref[...])).astype(o_ref.dtype)


def _merge(ya, yb, wa, wb, proj, gate_bias, *, tm=1024, tn=1024):
    t, d = ya.shape
    n = wa.shape[1]
    ga0 = PC_GATE // tn
    gb0 = (PC_GATE + n) // tn
    return pl.pallas_call(
        _merge_kernel,
        grid=(t // tm, n // tn),
        in_specs=[
            pl.BlockSpec((tm, d), lambda i, j: (i, 0)),
            pl.BlockSpec((tm, d), lambda i, j: (i, 0)),
            pl.BlockSpec((d, tn), lambda i, j: (0, j)),
            pl.BlockSpec((d, tn), lambda i, j: (0, j)),
            pl.BlockSpec((tm, tn), lambda i, j: (i, ga0 + j)),
            pl.BlockSpec((tm, tn), lambda i, j: (i, gb0 + j)),
            pl.BlockSpec((1, tn), lambda i, j: (0, j)),
            pl.BlockSpec((1, tn), lambda i, j: (0, n // tn + j)),
        ],
        out_specs=pl.BlockSpec((tm, tn), lambda i, j: (i, j)),
        out_shape=jax.ShapeDtypeStruct((t, n), BF16),
        compiler_params=_cparams("parallel", "arbitrary"),
        name="merge",
    )(ya, yb, wa, wb, proj, proj, gate_bias, gate_bias)


def _matmul_resid_kernel(a_ref, w_ref, r_ref, o_ref):
    o_ref[...] = r_ref[...] + _dot(a_ref[...], w_ref[...])


def _matmul_resid(a, w, resid, *, tm, tn, name):
    t, k = a.shape
    n = w.shape[1]
    return pl.pallas_call(
        _matmul_resid_kernel,
        grid=(t // tm, n // tn),
        in_specs=[
            pl.BlockSpec((tm, k), lambda i, j: (i, 0)),
            pl.BlockSpec((k, tn), lambda i, j: (0, j)),
            pl.BlockSpec((tm, tn), lambda i, j: (i, j)),
        ],
        out_specs=pl.BlockSpec((tm, tn), lambda i, j: (i, j)),
        out_shape=jax.ShapeDtypeStruct((t, n), F32),
        compiler_params=_cparams("parallel", "arbitrary"),
        name=name,
    )(a, w, resid)


def _ffn_gu_kernel(x_ref, g_ref, wg_ref, wu_ref, o_ref, h_ref):
    @pl.when(pl.program_id(1) == 0)
    def _():
        h_ref[...] = _rmsnorm_rows(x_ref[...], g_ref[...]).astype(BF16)

    h = h_ref[...]
    o_ref[...] = (_silu(_dot(h, wg_ref[...])) * _dot(h, wu_ref[...])).astype(o_ref.dtype)


def _ffn_gu(x, g, w_gu, *, tm=1024, tf=512):
    t, d = x.shape
    f = w_gu.shape[1] // 2
    nf = f // tf
    return pl.pallas_call(
        _ffn_gu_kernel,
        grid=(t // tm, nf),
        in_specs=[
            pl.BlockSpec((tm, d), lambda i, j: (i, 0)),
            pl.BlockSpec((1, d), lambda i, j: (0, 0)),
            pl.BlockSpec((d, tf), lambda i, j: (0, j)),
            pl.BlockSpec((d, tf), lambda i, j: (0, nf + j)),
        ],
        out_specs=pl.BlockSpec((tm, tf), lambda i, j: (i, j)),
        out_shape=jax.ShapeDtypeStruct((t, f), BF16),
        scratch_shapes=[pltpu.VMEM((tm, d), BF16)],
        compiler_params=_cparams("parallel", "arbitrary"),
        name="ffn_gu",
    )(x, g, w_gu, w_gu)


def _router_kernel(x_ref, g_ref, wr_ref, br_ref, h_ref, r_ref):
    h = _rmsnorm_rows(x_ref[...], g_ref[...])
    h_ref[...] = h
    h1, h2, h3 = _split3(h)
    w = wr_ref[...]
    w1, w2, w3 = _split3(w)
    logits = (_dot(h1, w1) + (_dot(h1, w2) + _dot(h2, w1))
              + (_dot(h1, w3) + _dot(h2, w2) + _dot(h3, w1))) + br_ref[...]
    lane = lax.broadcasted_iota(jnp.int32, logits.shape, 1)
    logits = jnp.where(lane < N_EXPERTS, logits, -jnp.inf)
    m1 = jnp.max(logits, axis=-1, keepdims=True)
    i1 = jnp.min(jnp.where(logits == m1, lane, SM_W), axis=-1, keepdims=True)
    rest = jnp.where(lane == i1, -jnp.inf, logits)
    m2 = jnp.max(rest, axis=-1, keepdims=True)
    i2 = jnp.min(jnp.where(rest == m2, lane, SM_W), axis=-1, keepdims=True)
    e = jnp.exp(m2 - m1)
    p1 = 1.0 / (1.0 + e)
    p2 = e / (1.0 + e)
    r_ref[...] = jnp.where(lane == 0, i1.astype(F32),
                           jnp.where(lane == 1, i2.astype(F32),
                                     jnp.where(lane == 2, p1, jnp.where(lane == 3, p2, 0.0))))


def _router(x, g, w_router, b_router, *, tm=512):
    t, d = x.shape
    return pl.pallas_call(
        _router_kernel,
        grid=(t // tm,),
        in_specs=[
            pl.BlockSpec((tm, d), lambda i: (i, 0)),
            pl.BlockSpec((1, d), lambda i: (0, 0)),
            pl.BlockSpec((d, SM_W), lambda i: (0, 0)),
            pl.BlockSpec((1, SM_W), lambda i: (0, 0)),
        ],
        out_specs=[
            pl.BlockSpec((tm, d), lambda i: (i, 0)),
            pl.BlockSpec((tm, SM_W), lambda i: (i, 0)),
        ],
        out_shape=[
            jax.ShapeDtypeStruct((t, d), F32),
            jax.ShapeDtypeStruct((t, SM_W), F32),
        ],
        compiler_params=_cparams("parallel"),
        name="router",
    )(x, g, w_router, b_router)


def _row_copy(src_ref, dst_ref, sem, src_row, dst_row):
    return pltpu.make_async_copy(src_ref.at[pl.ds(src_row, 1), :], dst_ref.at[pl.ds(dst_row, 1), :], sem)


def _gather_kernel(idx_ref, src_ref, o_ref, buf_ref, sem):
    rows = buf_ref.shape[0]
    base = pl.program_id(0) * rows

    def start(r, carry):
        _row_copy(src_ref, buf_ref, sem, idx_ref[base + r], r).start()
        return carry

    lax.fori_loop(0, rows, start, 0)

    def wait(r, carry):
        _row_copy(src_ref, buf_ref, sem, 0, r).wait()
        return carry

    lax.fori_loop(0, rows, wait, 0)
    o_ref[...] = buf_ref[...].astype(o_ref.dtype)


def _gather_rows(src, idx, *, rows=256):
    p = idx.shape[0]
    d = src.shape[1]
    return pl.pallas_call(
        _gather_kernel,
        grid_spec=pltpu.PrefetchScalarGridSpec(
            num_scalar_prefetch=1,
            grid=(p // rows,),
            in_specs=[pl.BlockSpec(memory_space=pl.ANY)],
            out_specs=pl.BlockSpec((rows, d), lambda i, idx_ref: (i, 0)),
            scratch_shapes=[pltpu.VMEM((rows, d), src.dtype), pltpu.SemaphoreType.DMA(())],
        ),
        out_shape=jax.ShapeDtypeStruct((p, d), BF16),
        compiler_params=_cparams("arbitrary"),
        name="moe_gather",
    )(idx, src)


def _moe_gu_kernel(te_ref, nu_ref, a_ref, wg_ref, wu_ref, o_ref):
    used = pl.program_id(0) < nu_ref[0]

    @pl.when(used)
    def _():
        a = a_ref[...]
        o_ref[...] = (_silu(_dot(a, wg_ref[0])) * _dot(a, wu_ref[0])).astype(o_ref.dtype)

    @pl.when(jnp.logical_not(used))
    def _():
        o_ref[...] = jnp.zeros_like(o_ref)


def _moe_gu(a, w_gu, tile_expert, n_used, *, tm, tf=1024):
    p, d = a.shape
    f = w_gu.shape[2] // 2
    nf = f // tf
    return pl.pallas_call(
        _moe_gu_kernel,
        grid_spec=pltpu.PrefetchScalarGridSpec(
            num_scalar_prefetch=2,
            grid=(p // tm, nf),
            in_specs=[
                pl.BlockSpec((tm, d), lambda i, j, te, nu: (i, 0)),
                pl.BlockSpec((1, d, tf), lambda i, j, te, nu: (te[i], 0, j)),
                pl.BlockSpec((1, d, tf), lambda i, j, te, nu: (te[i], 0, nf + j)),
            ],
            out_specs=pl.BlockSpec((tm, tf), lambda i, j, te, nu: (i, j)),
        ),
        out_shape=jax.ShapeDtypeStruct((p, f), BF16),
        compiler_params=_cparams("parallel", "arbitrary"),
        name="moe_gu",
    )(tile_expert, n_used, a, w_gu, w_gu)


def _moe_down_kernel(te_ref, nu_ref, a_ref, w_ref, o_ref):
    used = pl.program_id(0) < nu_ref[0]

    @pl.when(used)
    def _():
        o_ref[...] = _dot(a_ref[...], w_ref[0])

    @pl.when(jnp.logical_not(used))
    def _():
        o_ref[...] = jnp.zeros_like(o_ref)


def _moe_down(a, w_down, tile_expert, n_used, *, tm, tn=1024):
    p, f = a.shape
    n = w_down.shape[2]
    return pl.pallas_call(
        _moe_down_kernel,
        grid_spec=pltpu.PrefetchScalarGridSpec(
            num_scalar_prefetch=2,
            grid=(p // tm, n // tn),
            in_specs=[
                pl.BlockSpec((tm, f), lambda i, j, te, nu: (i, 0)),
                pl.BlockSpec((1, f, tn), lambda i, j, te, nu: (te[i], 0, j)),
            ],
            out_specs=pl.BlockSpec((tm, tn), lambda i, j, te, nu: (i, j)),
        ),
        out_shape=jax.ShapeDtypeStruct((p, n), F32),
        compiler_params=_cparams("parallel", "arbitrary"),
        name="moe_down",
    )(tile_expert, n_used, a, w_down)


def _combine_kernel(p1_ref, p2_ref, y_ref, x_ref, r_ref, o_ref, b1_ref, b2_ref, sem):
    rows = b1_ref.shape[0]
    base = pl.program_id(0) * rows

    def start(r, carry):
        _row_copy(y_ref, b1_ref, sem.at[0], p1_ref[base + r], r).start()
        _row_copy(y_ref, b2_ref, sem.at[1], p2_ref[base + r], r).start()
        return carry

    lax.fori_loop(0, rows, start, 0)

    def wait(r, carry):
        _row_copy(y_ref, b1_ref, sem.at[0], 0, r).wait()
        _row_copy(y_ref, b2_ref, sem.at[1], 0, r).wait()
        return carry

    lax.fori_loop(0, rows, wait, 0)
    route = r_ref[...]
    o_ref[...] = x_ref[...] + route[:, 2:3] * b1_ref[...] + route[:, 3:4] * b2_ref[...]


def _combine(y_sorted, x, route, pos1, pos2, *, rows=256):
    t, d = x.shape
    return pl.pallas_call(
        _combine_kernel,
        grid_spec=pltpu.PrefetchScalarGridSpec(
            num_scalar_prefetch=2,
            grid=(t // rows,),
            in_specs=[
                pl.BlockSpec(memory_space=pl.ANY),
                pl.BlockSpec((rows, d), lambda i, p1, p2: (i, 0)),
                pl.BlockSpec((rows, SM_W), lambda i, p1, p2: (i, 0)),
            ],
            out_specs=pl.BlockSpec((rows, d), lambda i, p1, p2: (i, 0)),
            scratch_shapes=[pltpu.VMEM((rows, d), F32), pltpu.VMEM((rows, d), F32),
                            pltpu.SemaphoreType.DMA((2,))],
        ),
        out_shape=jax.ShapeDtypeStruct((t, d), F32),
        compiler_params=_cparams("arbitrary"),
        name="moe_combine",
    )(pos1, pos2, y_sorted, x, route)


MOE_TM = 512


def _moe_plan(route, tm):
    t = route.shape[0]
    n_tiles = (t * TOP_K) // tm + N_EXPERTS
    eid = route[:, :TOP_K].astype(jnp.int32).reshape(-1)
    onehot = (eid[:, None] == jnp.arange(N_EXPERTS, dtype=jnp.int32)[None, :]).astype(jnp.int32)
    csum = jnp.cumsum(onehot, axis=0)
    rank = jnp.sum((csum - onehot) * onehot, axis=1)
    counts = csum[-1]
    tiles_per = (counts + tm - 1) // tm
    tile_end = jnp.cumsum(tiles_per)
    start = (tile_end - tiles_per) * tm
    pos = (jnp.sum(onehot * start[None, :], axis=1) + rank).astype(jnp.int32)
    n_used = tile_end[-1].astype(jnp.int32)
    tile_id = jnp.arange(n_tiles, dtype=jnp.int32)
    tile_expert = jnp.sum((tile_id[:, None] >= tile_end[None, :]).astype(jnp.int32), axis=1)
    tile_expert = jnp.minimum(tile_expert, N_EXPERTS - 1).astype(jnp.int32)
    last_expert = jnp.max(jnp.where(counts > 0, jnp.arange(N_EXPERTS, dtype=jnp.int32), 0))
    tile_expert = jnp.where(tile_id < n_used, tile_expert, last_expert).astype(jnp.int32)
    token = jnp.repeat(jnp.arange(t, dtype=jnp.int32), TOP_K)
    row_token = jnp.zeros((n_tiles * tm,), jnp.int32).at[pos].set(token)
    pos = pos.reshape(t, TOP_K)
    return row_token, tile_expert, n_used.reshape(1), pos[:, 0], pos[:, 1]


def _moe_ffn(x, g, w_router, b_router, w_gu, w_down):
    h, route = _router(x, g, w_router, b_router)
    row_token, tile_expert, n_used, pos1, pos2 = _moe_plan(route, MOE_TM)
    a = _gather_rows(h, row_token)
    act = _moe_gu(a, w_gu, tile_expert, n_used, tm=MOE_TM)
    y = _moe_down(act, w_down, tile_expert, n_used, tm=MOE_TM)
    return _combine(y, x, route, pos1, pos2)


def _final_norm_kernel(x_ref, g_ref, o_ref):
    o_ref[...] = _rmsnorm_rows(x_ref[...], g_ref[...])


def _final_norm(x, g, *, tm=512):
    t, d = x.shape
    return pl.pallas_call(
        _final_norm_kernel,
        grid=(t // tm,),
        in_specs=[pl.BlockSpec((tm, d), lambda i: (i, 0)), pl.BlockSpec((1, d), lambda i: (0, 0))],
        out_specs=pl.BlockSpec((tm, d), lambda i: (i, 0)),
        out_shape=jax.ShapeDtypeStruct((t, d), F32),
        compiler_params=_cparams("parallel"),
        name="final_norm",
    )(x, g)


def _pack_w_in(w):
    sizes = (ML_QK, ML_QK, ML_WIDTH, ML_WIDTH, ML_HEADS, ML_HEADS, SSD_WIDTH, SSD_CONV, SSD_HEADS,
             2 * D_MODEL)
    offs = [0]
    for s in sizes:
        offs.append(offs[-1] + s)
    q, k, v, o, i_g, f_g, z, xbc, dt, gate = (w[:, offs[n]:offs[n + 1]] for n in range(len(sizes)))
    main = jnp.concatenate([q, k, v, o, z, xbc, gate], axis=1).astype(BF16)
    pad1 = jnp.zeros((w.shape[0], SM_DT - 2 * ML_HEADS), w.dtype)
    pad2 = jnp.zeros((w.shape[0], SM_W - SM_DT - SSD_HEADS), w.dtype)
    small = jnp.concatenate([i_g, f_g, pad1, dt, pad2], axis=1).astype(BF16)
    return main, small, small.T


def _lane_row(vec, offset):
    return jnp.zeros((1, SM_W), F32).at[0, offset:offset + vec.shape[0]].set(vec.astype(F32))


def _mixer(x, l, bsz, seq, p, consts):
    tril, triu, expand = consts
    w_main, w_small, w_small_t = _pack_w_in(p["w_in"][l])
    proj, small, small_t = _proj(x, p["norm_mix_g"][l][None, :], w_main, w_small, w_small_t)

    if_bias = p["if_bias"][l]
    y_a = _mlstm(proj, small, small_t, p["qk_conv_w"][l], p["qk_conv_b"][l][None, :],
                 _lane_row(if_bias, 0),
                 jnp.broadcast_to(if_bias.astype(F32)[:, None], (2 * ML_HEADS, SEQ_CHUNK)),
                 p["mh_norm_g"][l][None, :], tril, triu, bsz=bsz, seq=seq)
    y_b = _ssd(proj, small, small_t, p["ssd_conv_w"][l], p["ssd_conv_b"][l][None, :],
               _lane_row(p["dt_bias"][l], SM_DT),
               jnp.broadcast_to(p["dt_bias"][l].astype(F32)[:, None], (SSD_HEADS, SEQ_CHUNK)),
               _lane_row(p["a_log"][l], SM_DT),
               jnp.broadcast_to(p["a_log"][l].astype(F32)[:, None], (SSD_HEADS, SEQ_CHUNK)),
               jnp.repeat(p["d_skip"][l].astype(F32), SSD_P)[None, :],
               p["ssd_norm_g"][l][None, :], tril, triu, expand, bsz=bsz, seq=seq)
    merged = _merge(y_a, y_b, p["w_branch_a"][l].astype(BF16), p["w_branch_b"][l].astype(BF16), proj,
                    p["gate_bias"][l][None, :])
    return _matmul_resid(merged, p["w_out"][l].astype(BF16), x, tm=1024, tn=1024, name="out_proj")


def kernel(x, norm_mix_g, w_in, if_bias, qk_conv_w, qk_conv_b, mh_norm_g, ssd_conv_w, ssd_conv_b,
           dt_bias, a_log, d_skip, ssd_norm_g, gate_bias, w_branch_a, w_branch_b, w_out, norm_ffn_g,
           ffn_w_gu, ffn_w_down, router_w, router_b, exp_w_gu, exp_w_down, norm_final_g):
    p = dict(norm_mix_g=norm_mix_g, w_in=w_in, if_bias=if_bias, qk_conv_w=qk_conv_w,
             qk_conv_b=qk_conv_b, mh_norm_g=mh_norm_g, ssd_conv_w=ssd_conv_w, ssd_conv_b=ssd_conv_b,
             dt_bias=dt_bias, a_log=a_log, d_skip=d_skip, ssd_norm_g=ssd_norm_g, gate_bias=gate_bias,
             w_branch_a=w_branch_a, w_branch_b=w_branch_b, w_out=w_out)
    bsz, seq, d = x.shape
    x = x.reshape(bsz * seq, d)

    idx = jnp.arange(SEQ_CHUNK, dtype=jnp.int32)
    tril = (idx[:, None] >= idx[None, :]).astype(BF16)
    triu = (idx[:, None] <= idx[None, :]).astype(BF16)
    lane = jnp.arange(SM_W, dtype=jnp.int32)[:, None]
    chan = jnp.arange(SSD_WIDTH, dtype=jnp.int32)[None, :]
    expand = (lane == SM_DT + chan // SSD_P).astype(BF16)
    consts = (tril, triu, expand)

    for l in range(DEPTH):
        x = _mixer(x, l, bsz, seq, p, consts)
        g = norm_ffn_g[l][None, :]
        if l % 2 == 0:
            act = _ffn_gu(x, g, ffn_w_gu[l // 2].astype(BF16))
            x = _matmul_resid(act, ffn_w_down[l // 2].astype(BF16), x, tm=512, tn=1024, name="ffn_down")
        else:
            w_r = jnp.pad(router_w[l // 2], ((0, 0), (0, SM_W - N_EXPERTS)))
            b_r = _lane_row(router_b[l // 2], 0)
            x = _moe_ffn(x, g, w_r, b_r, exp_w_gu[l // 2].astype(BF16), exp_w_down[l // 2].astype(BF16))
    return _final_norm(x, norm_final_g[None, :]).reshape(bsz, seq, d)
```

```python
import functools
import math

import jax
import jax.numpy as jnp
from jax import lax
from jax.experimental import pallas as pl
from jax.experimental.pallas import tpu as pltpu

F32 = jnp.float32
BF16 = jnp.bfloat16

D_MODEL = 2048
DEPTH = 4
EPS = 1e-6
CONV_K = 4
ML_HEADS = 4
ML_DV = 512
ML_DK = 256
ML_QK = ML_HEADS * ML_DK
ML_WIDTH = ML_HEADS * ML_DV
SSD_HEADS = 32
SSD_P = 64
SSD_GROUPS = 8
SSD_N = 128
SSD_R = SSD_HEADS // SSD_GROUPS
SSD_WIDTH = SSD_HEADS * SSD_P
SSD_GN = SSD_GROUPS * SSD_N
SSD_CONV = SSD_WIDTH + 2 * SSD_GN
SSD_GW = SSD_R * SSD_P
FFN_DENSE = 5632
N_EXPERTS = 8
TOP_K = 2
FFN_EXPERT = 4096

PC_Q = 0
PC_K = 1024
PC_V = 2048
PC_O = 4096
PC_Z = 6144
PC_XBC = 8192
PC_GATE = 12288
PC_TOTAL = 16384
SRC_IF = 6144
SRC_DT = 12296
SHIFT_ZXBC = 8
SHIFT_GATE = 40
PROJ_TN = 1024
SM_W = 128
SM_COLS = 2 * SM_W
SM_DT = SRC_DT - (SRC_DT // SM_W) * SM_W

V7X_VMEM_BYTES = 64 * 1024 * 1024
VMEM_LIMIT = 56 * 1024 * 1024

SEQ_CHUNK = 256
CARRY = 8


def _cparams(*sem):
    return pltpu.CompilerParams(dimension_semantics=sem, vmem_limit_bytes=VMEM_LIMIT)


def _sigmoid(x):
    return 1.0 / (1.0 + jnp.exp(-x))


def _silu(x):
    return x * _sigmoid(x)


def _softplus(x):
    return jnp.maximum(x, 0.0) + jnp.log(1.0 + jnp.exp(-jnp.abs(x)))


def _log_sigmoid(x):
    return jnp.minimum(x, 0.0) - jnp.log(1.0 + jnp.exp(-jnp.abs(x)))


def _split3(x):
    x1 = x.astype(BF16)
    r1 = x - x1.astype(F32)
    x2 = r1.astype(BF16)
    x3 = (r1 - x2.astype(F32)).astype(BF16)
    return x1, x2, x3


def _dot(a, b):
    return jnp.dot(a, b, preferred_element_type=F32)


def _dot_nt(a, b):
    return lax.dot_general(a, b, (((1,), (1,)), ((), ())), preferred_element_type=F32)


def _dot_tn(a, b):
    return lax.dot_general(a, b, (((0,), (0,)), ((), ())), preferred_element_type=F32)


def _dot_f32_left(x, m01):
    x1, x2, x3 = _split3(x)
    return _dot(x1, m01) + _dot(x2, m01) + _dot(x3, m01)


def _dot_f32_right(m01, x):
    x1, x2, x3 = _split3(x)
    return _dot(m01, x1) + _dot(m01, x2) + _dot(m01, x3)


def _rmsnorm_rows(x, g):
    ms = jnp.mean(x * x, axis=-1, keepdims=True)
    return (x * lax.rsqrt(ms + EPS)) * g


def _norm_small_kernel(x_ref, g_ref, w0_ref, w1_ref, h_ref, s_ref, st_ref):
    hb = _rmsnorm_rows(x_ref[...], g_ref[...]).astype(BF16)
    h_ref[...] = hb
    w = jnp.concatenate([w0_ref[0], w1_ref[0]], axis=1)
    s_ref[...] = _dot(hb, w.astype(BF16))
    st_ref[...] = _dot_nt(w.T.astype(BF16), hb)


def _norm_small(x, g, w_in, layer, *, tm=1024):
    t, d = x.shape
    return pl.pallas_call(
        _norm_small_kernel,
        grid=(t // tm,),
        in_specs=[
            pl.BlockSpec((tm, d), lambda i: (i, 0)),
            pl.BlockSpec((1, d), lambda i: (0, 0)),
            pl.BlockSpec((1, d, SM_W), lambda i: (layer, 0, SRC_IF // SM_W)),
            pl.BlockSpec((1, d, SM_W), lambda i: (layer, 0, SRC_DT // SM_W)),
        ],
        out_specs=[
            pl.BlockSpec((tm, d), lambda i: (i, 0)),
            pl.BlockSpec((tm, SM_COLS), lambda i: (i, 0)),
            pl.BlockSpec((SM_COLS, tm), lambda i: (0, i)),
        ],
        out_shape=[
            jax.ShapeDtypeStruct((t, d), BF16),
            jax.ShapeDtypeStruct((t, SM_COLS), F32),
            jax.ShapeDtypeStruct((SM_COLS, t), F32),
        ],
        compiler_params=_cparams("parallel"),
        name="norm_small",
    )(x, g, w_in, w_in)


CAST_ROWS = 256


def _proj_kernel(h_ref, wa_ref, wb_ref, o_ref, ws_ref):
    j = pl.program_id(0)

    @pl.when(pl.program_id(1) == 0)
    def _():
        tn = ws_ref.shape[1]
        width = tn + SM_W
        shift = jnp.where(j < PC_Z // tn, 0, jnp.where(j < PC_GATE // tn, SHIFT_ZXBC, SHIFT_GATE))
        amount = jnp.where(shift == 0, 0, width - shift)
        for r in range(0, ws_ref.shape[0], CAST_ROWS):
            both = jnp.concatenate([wa_ref[0, r:r + CAST_ROWS, :], wb_ref[0, r:r + CAST_ROWS, :]], axis=1)
            ws_ref[r:r + CAST_ROWS, :] = pltpu.roll(both, amount, axis=1)[:, :tn].astype(BF16)

    o_ref[...] = _dot(h_ref[...], ws_ref[...]).astype(o_ref.dtype)


def _proj(h, w_in, layer, *, tm=1024):
    t, d = h.shape
    tn = PROJ_TN
    return pl.pallas_call(
        _proj_kernel,
        grid=(PC_TOTAL // tn, t // tm),
        in_specs=[
            pl.BlockSpec((tm, d), lambda j, i: (i, 0)),
            pl.BlockSpec((1, d, tn), lambda j, i: (layer, 0, j)),
            pl.BlockSpec((1, d, SM_W), lambda j, i: (layer, 0, (j + 1) * (tn // SM_W))),
        ],
        out_specs=pl.BlockSpec((tm, tn), lambda j, i: (i, j)),
        out_shape=jax.ShapeDtypeStruct((t, PC_TOTAL), BF16),
        scratch_shapes=[pltpu.VMEM((d, tn), BF16)],
        compiler_params=_cparams("parallel", "arbitrary"),
        name="proj",
    )(h, w_in, w_in)


def _causal_conv(raw, xc_ref, carry_ref, w_ref, b_ref, first_chunk):
    length = raw.shape[0]

    @pl.when(first_chunk)
    def _():
        carry_ref[...] = jnp.zeros_like(carry_ref)

    xc_ref[0:CARRY, :] = carry_ref[...]
    xc_ref[CARRY:CARRY + length, :] = raw
    carry_ref[...] = raw[length - CARRY:, :]
    y = b_ref[...] + w_ref[CONV_K - 1:CONV_K, :] * raw
    for back in range(1, CONV_K):
        tap = xc_ref[CARRY - back:CARRY - back + length, :]
        y = y + w_ref[CONV_K - 1 - back:CONV_K - back, :] * tap
    return y


def _mlstm_kernel(q_ref, k_ref, v_ref, o_ref, sm_ref, smt_ref, cw_ref, cb_ref, ifr_ref, ifc_ref,
                  g_ref, tril_ref, triu_ref, y_ref, xc_ref, carry_ref, c_ref, n_ref, m_ref):
    length = q_ref.shape[0]
    first = pl.program_id(1) == 0

    @pl.when(first)
    def _():
        c_ref[...] = jnp.zeros_like(c_ref)
        n_ref[...] = jnp.zeros_like(n_ref)
        m_ref[...] = jnp.zeros_like(m_ref)

    raw = jnp.concatenate([q_ref[...], k_ref[...]], axis=-1).astype(F32)
    qk = _silu(_causal_conv(raw, xc_ref, carry_ref, cw_ref, cb_ref, first))

    cols = sm_ref[...] + ifr_ref[...]
    rows = smt_ref[...] + ifc_ref[...]
    b_cols = _dot_f32_right(tril_ref[...], _log_sigmoid(cols))
    b_rows = _dot_f32_left(_log_sigmoid(rows), triu_ref[...])
    r_i = lax.broadcasted_iota(jnp.int32, (length, length), 0)
    c_i = lax.broadcasted_iota(jnp.int32, (length, length), 1)
    causal = r_i >= c_i

    for h in range(ML_HEADS):
        q_h = qk[:, h * ML_DK:(h + 1) * ML_DK] * (ML_DK ** -0.5)
        k_h = qk[:, ML_QK + h * ML_DK:ML_QK + (h + 1) * ML_DK]
        qb = q_h.astype(BF16)
        kb = k_h.astype(BF16)
        v_h = v_ref[:, h * ML_DV:(h + 1) * ML_DV]
        i_col = cols[:, h:h + 1]
        b_col = b_cols[:, ML_HEADS + h:ML_HEADS + h + 1]
        i_row = rows[h:h + 1, :]
        b_row = b_rows[ML_HEADS + h:ML_HEADS + h + 1, :]
        m_prev = m_ref[h:h + 1, 0:1]
        c_prev = c_ref[h]
        n_prev = n_ref[h:h + 1, :]

        log_d = jnp.where(causal, b_col - b_row + i_row, -jnp.inf)
        m_inter = b_col + m_prev
        m_t = jnp.maximum(m_inter, jnp.max(log_d, axis=-1, keepdims=True))
        scores = _dot_nt(qb, kb) * jnp.exp(log_d - m_t)
        inter = jnp.exp(m_inter - m_t)
        num = _dot(scores.astype(BF16), v_h) + inter * _dot(qb, c_prev.astype(BF16))
        den = (jnp.sum(scores, axis=-1, keepdims=True)
               + inter * jnp.sum(q_h * n_prev, axis=-1, keepdims=True))
        hh = num / jnp.maximum(jnp.abs(den), jnp.exp(-m_t))
        hh = hh * lax.rsqrt(jnp.mean(hh * hh, axis=-1, keepdims=True) + EPS)
        gate = _sigmoid(o_ref[:, h * ML_DV:(h + 1) * ML_DV].astype(F32))
        y_ref[:, h * ML_DV:(h + 1) * ML_DV] = (
            hh * g_ref[:, h * ML_DV:(h + 1) * ML_DV] * gate).astype(y_ref.dtype)

        b_last = b_col[length - 1:length, :]
        log_w_row = b_last - b_row + i_row
        m_new = jnp.maximum(b_last + m_prev, jnp.max(log_w_row, axis=-1, keepdims=True))
        w_col = jnp.exp(b_last - b_col + i_col - m_new)
        decay = jnp.exp(b_last + m_prev - m_new)
        vw = (v_h.astype(F32) * w_col).astype(BF16)
        c_ref[h] = decay * c_prev + _dot_tn(kb, vw)
        n_ref[h:h + 1, :] = decay * n_prev + jnp.sum(k_h * w_col, axis=0, keepdims=True)
        m_ref[h:h + 1, :] = jnp.broadcast_to(m_new, (1, m_ref.shape[1]))


def _mlstm(proj, small, small_t, conv_w, conv_b, if_row, if_col, norm_g, tril, triu, *, bsz, seq):
    length = SEQ_CHUNK
    nc = seq // length
    t = bsz * seq
    row = lambda b, c: b * nc + c
    return pl.pallas_call(
        _mlstm_kernel,
        grid=(bsz, nc),
        in_specs=[
            pl.BlockSpec((length, ML_QK), lambda b, c: (row(b, c), PC_Q // ML_QK)),
            pl.BlockSpec((length, ML_QK), lambda b, c: (row(b, c), PC_K // ML_QK)),
            pl.BlockSpec((length, ML_WIDTH), lambda b, c: (row(b, c), PC_V // ML_WIDTH)),
            pl.BlockSpec((length, ML_WIDTH), lambda b, c: (row(b, c), PC_O // ML_WIDTH)),
            pl.BlockSpec((length, SM_W), lambda b, c: (row(b, c), 0)),
            pl.BlockSpec((2 * ML_HEADS, length), lambda b, c: (0, row(b, c))),
            pl.BlockSpec((CONV_K, 2 * ML_QK), lambda b, c: (0, 0)),
            pl.BlockSpec((1, 2 * ML_QK), lambda b, c: (0, 0)),
            pl.BlockSpec((1, SM_W), lambda b, c: (0, 0)),
            pl.BlockSpec((2 * ML_HEADS, length), lambda b, c: (0, 0)),
            pl.BlockSpec((1, ML_WIDTH), lambda b, c: (0, 0)),
            pl.BlockSpec((length, length), lambda b, c: (0, 0)),
            pl.BlockSpec((length, length), lambda b, c: (0, 0)),
        ],
        out_specs=pl.BlockSpec((length, ML_WIDTH), lambda b, c: (row(b, c), 0)),
        out_shape=jax.ShapeDtypeStruct((t, ML_WIDTH), BF16),
        scratch_shapes=[
            pltpu.VMEM((CARRY + length, 2 * ML_QK), F32),
            pltpu.VMEM((CARRY, 2 * ML_QK), F32),
            pltpu.VMEM((ML_HEADS, ML_DK, ML_DV), F32),
            pltpu.VMEM((2 * ML_HEADS, ML_DK), F32),
            pltpu.VMEM((2 * ML_HEADS, 128), F32),
        ],
        compiler_params=_cparams("parallel", "arbitrary"),
        name="mlstm",
    )(proj, proj, proj, proj, small, small_t, conv_w, conv_b, if_row, if_col, norm_g, tril, triu)


def _ssd_kernel(xbc_ref, z_ref, sm_ref, smt_ref, cw_ref, cb_ref, dtb_r_ref, dtb_c_ref, al_r_ref,
                al_c_ref, dsk_ref, g_ref, tril_ref, triu_ref, e_ref, y_ref,
                xc_ref, carry_ref, st_ref):
    length = xbc_ref.shape[0]
    first = pl.program_id(1) == 0

    @pl.when(first)
    def _():
        st_ref[...] = jnp.zeros_like(st_ref)

    xconv = _silu(_causal_conv(xbc_ref[...].astype(F32), xc_ref, carry_ref, cw_ref, cb_ref, first))

    lane = lax.broadcasted_iota(jnp.int32, (1, SM_W), 1)
    head_lane = jnp.logical_and(lane >= SM_DT, lane < SM_DT + SSD_HEADS)
    dt_c = jnp.where(head_lane, _softplus(sm_ref[...] + dtb_r_ref[...]), 0.0)
    dt_r = _softplus(smt_ref[SM_DT:SM_DT + SSD_HEADS, :] + dtb_c_ref[...])
    a_c = dt_c * (-jnp.exp(al_r_ref[...]))
    a_r = dt_r * (-jnp.exp(al_c_ref[...]))
    acum_c = _dot_f32_right(tril_ref[...], a_c)
    acum_r = _dot_f32_left(a_r, triu_ref[...])
    a_last = acum_c[length - 1:length, :]

    e01 = e_ref[...]
    dt_e = _dot_f32_left(dt_c, e01)
    from_start_e = _dot_f32_left(jnp.exp(acum_c), e01)
    to_end_e = _dot_f32_left(jnp.exp(a_last - acum_c), e01)
    chunk_e = _dot_f32_left(jnp.broadcast_to(jnp.exp(a_last), (8, SM_W)), e01)[0:1, :]

    r_i = lax.broadcasted_iota(jnp.int32, (length, length), 0)
    c_i = lax.broadcasted_iota(jnp.int32, (length, length), 1)
    causal = r_i >= c_i

    for g in range(SSD_GROUPS):
        ch = slice(g * SSD_GW, (g + 1) * SSD_GW)
        xs_g = xconv[:, ch]
        bm = xconv[:, SSD_WIDTH + g * SSD_N:SSD_WIDTH + (g + 1) * SSD_N].astype(BF16)
        cm = xconv[:, SSD_WIDTH + SSD_GN + g * SSD_N:SSD_WIDTH + SSD_GN + (g + 1) * SSD_N].astype(BF16)
        xdt = xs_g * dt_e[:, ch]
        cb = _dot_nt(cm, bm)
        prev = st_ref[g]
        y = _dot(cm, prev.astype(BF16)) * from_start_e[:, ch] + dsk_ref[:, ch] * xs_g
        diag = []
        for r in range(SSD_R):
            h = g * SSD_R + r
            seg = acum_c[:, SM_DT + h:SM_DT + h + 1] - acum_r[h:h + 1, :]
            att = cb * jnp.exp(jnp.where(causal, seg, -jnp.inf))
            diag.append(_dot(att.astype(BF16), xdt[:, r * SSD_P:(r + 1) * SSD_P].astype(BF16)))
        y = y + jnp.concatenate(diag, axis=-1)
        st_ref[g] = prev * chunk_e[:, ch] + _dot_tn(bm, (xdt * to_end_e[:, ch]).astype(BF16))
        y = y * _silu(z_ref[:, ch].astype(F32))
        y = y * lax.rsqrt(jnp.mean(y * y, axis=-1, keepdims=True) + EPS)
        y_ref[:, ch] = (y * g_ref[:, ch]).astype(y_ref.dtype)


def _ssd(proj, small, small_t, conv_w, conv_b, dtb_row, dtb_col, al_row, al_col, dskip_e, norm_g,
         tril, triu, expand, *, bsz, seq):
    length = SEQ_CHUNK
    nc = seq // length
    t = bsz * seq
    row = lambda b, c: b * nc + c
    const = lambda b, c: (0, 0)
    return pl.pallas_call(
        _ssd_kernel,
        grid=(bsz, nc),
        in_specs=[
            pl.BlockSpec((length, SSD_CONV), lambda b, c: (row(b, c), PC_XBC // SSD_CONV)),
            pl.BlockSpec((length, SSD_WIDTH), lambda b, c: (row(b, c), PC_Z // SSD_WIDTH)),
            pl.BlockSpec((length, SM_W), lambda b, c: (row(b, c), 1)),
            pl.BlockSpec((2 * SSD_HEADS, length), lambda b, c: (SM_W // (2 * SSD_HEADS), row(b, c))),
            pl.BlockSpec((CONV_K, SSD_CONV), const),
            pl.BlockSpec((1, SSD_CONV), const),
            pl.BlockSpec((1, SM_W), const),
            pl.BlockSpec((SSD_HEADS, length), const),
            pl.BlockSpec((1, SM_W), const),
            pl.BlockSpec((SSD_HEADS, length), const),
            pl.BlockSpec((1, SSD_WIDTH), const),
            pl.BlockSpec((1, SSD_WIDTH), const),
            pl.BlockSpec((length, length), const),
            pl.BlockSpec((length, length), const),
            pl.BlockSpec((SM_W, SSD_WIDTH), const),
        ],
        out_specs=pl.BlockSpec((length, SSD_WIDTH), lambda b, c: (row(b, c), 0)),
        out_shape=jax.ShapeDtypeStruct((t, SSD_WIDTH), BF16),
        scratch_shapes=[
            pltpu.VMEM((CARRY + length, SSD_CONV), F32),
            pltpu.VMEM((CARRY, SSD_CONV), F32),
            pltpu.VMEM((SSD_GROUPS, SSD_N, SSD_GW), F32),
        ],
        compiler_params=_cparams("parallel", "arbitrary"),
        name="ssd",
    )(proj, proj, small, small_t, conv_w, conv_b, dtb_row, dtb_col, al_row, al_col, dskip_e, norm_g,
      tril, triu, expand)


def _cast_weight(w_ref, ws_ref):
    lead = (0,) * (len(w_ref.shape) - 2)
    for r in range(0, ws_ref.shape[0], CAST_ROWS):
        ws_ref[r:r + CAST_ROWS, :] = w_ref[lead + (slice(r, r + CAST_ROWS), slice(None))].astype(BF16)


def _merge_kernel(ya_ref, yb_ref, wa_ref, wb_ref, ga_ref, gb_ref, ba_ref, bb_ref, o_ref, was_ref, wbs_ref):
    @pl.when(pl.program_id(1) == 0)
    def _():
        _cast_weight(wa_ref, was_ref)
        _cast_weight(wb_ref, wbs_ref)

    ga = _sigmoid(ga_ref[...].astype(F32) + ba_ref[...])
    gb = _sigmoid(gb_ref[...].astype(F32) + bb_ref[...])
    o_ref[...] = (ga * _dot(ya_ref[...], was_ref[...])
                  + gb * _dot(yb_ref[...], wbs_ref[...])).astype(o_ref.dtype)


def _merge(ya, yb, wa, wb, layer, proj, gate_bias, *, tm=1024, tn=512):
    t, d = ya.shape
    n = wa.shape[2]
    ga0 = PC_GATE // tn
    gb0 = (PC_GATE + n) // tn
    return pl.pallas_call(
        _merge_kernel,
        grid=(n // tn, t // tm),
        in_specs=[
            pl.BlockSpec((tm, d), lambda j, i: (i, 0)),
            pl.BlockSpec((tm, d), lambda j, i: (i, 0)),
            pl.BlockSpec((1, d, tn), lambda j, i: (layer, 0, j)),
            pl.BlockSpec((1, d, tn), lambda j, i: (layer, 0, j)),
            pl.BlockSpec((tm, tn), lambda j, i: (i, ga0 + j)),
            pl.BlockSpec((tm, tn), lambda j, i: (i, gb0 + j)),
            pl.BlockSpec((1, tn), lambda j, i: (0, j)),
            pl.BlockSpec((1, tn), lambda j, i: (0, n // tn + j)),
        ],
        out_specs=pl.BlockSpec((tm, tn), lambda j, i: (i, j)),
        out_shape=jax.ShapeDtypeStruct((t, n), BF16),
        scratch_shapes=[pltpu.VMEM((d, tn), BF16), pltpu.VMEM((d, tn), BF16)],
        compiler_params=_cparams("parallel", "arbitrary"),
        name="merge",
    )(ya, yb, wa, wb, proj, proj, gate_bias, gate_bias)


def _matmul_resid_kernel(a_ref, w_ref, r_ref, o_ref, ws_ref):
    @pl.when(pl.program_id(1) == 0)
    def _():
        _cast_weight(w_ref, ws_ref)

    o_ref[...] = r_ref[...] + _dot(a_ref[...], ws_ref[...])


def _matmul_resid(a, w, layer, resid, *, tm, tn, name):
    t, k = a.shape
    n = w.shape[2]
    return pl.pallas_call(
        _matmul_resid_kernel,
        grid=(n // tn, t // tm),
        in_specs=[
            pl.BlockSpec((tm, k), lambda j, i: (i, 0)),
            pl.BlockSpec((1, k, tn), lambda j, i: (layer, 0, j)),
            pl.BlockSpec((tm, tn), lambda j, i: (i, j)),
        ],
        out_specs=pl.BlockSpec((tm, tn), lambda j, i: (i, j)),
        out_shape=jax.ShapeDtypeStruct((t, n), F32),
        scratch_shapes=[pltpu.VMEM((k, tn), BF16)],
        compiler_params=_cparams("parallel", "arbitrary"),
        name=name,
    )(a, w, resid)


def _ffn_gu_kernel(x_ref, g_ref, wg_ref, wu_ref, o_ref, h_ref):
    @pl.when(pl.program_id(1) == 0)
    def _():
        h_ref[...] = _rmsnorm_rows(x_ref[...], g_ref[...]).astype(BF16)

    h = h_ref[...]
    gate = _dot(h, wg_ref[0].astype(BF16))
    up = _dot(h, wu_ref[0].astype(BF16))
    o_ref[...] = (_silu(gate) * up).astype(o_ref.dtype)


def _ffn_gu(x, g, w_gu, layer, *, tm=1024, tf=512):
    t, d = x.shape
    f = w_gu.shape[2] // 2
    nf = f // tf
    return pl.pallas_call(
        _ffn_gu_kernel,
        grid=(t // tm, nf),
        in_specs=[
            pl.BlockSpec((tm, d), lambda i, j: (i, 0)),
            pl.BlockSpec((1, d), lambda i, j: (0, 0)),
            pl.BlockSpec((1, d, tf), lambda i, j: (layer, 0, j)),
            pl.BlockSpec((1, d, tf), lambda i, j: (layer, 0, nf + j)),
        ],
        out_specs=pl.BlockSpec((tm, tf), lambda i, j: (i, j)),
        out_shape=jax.ShapeDtypeStruct((t, f), BF16),
        scratch_shapes=[pltpu.VMEM((tm, d), BF16)],
        compiler_params=_cparams("parallel", "arbitrary"),
        name="ffn_gu",
    )(x, g, w_gu, w_gu)


def _router_kernel(x_ref, g_ref, wr_ref, br_ref, h_ref, r_ref):
    h = _rmsnorm_rows(x_ref[...], g_ref[...])
    h_ref[...] = h
    h1, h2, h3 = _split3(h)
    w = wr_ref[...]
    w1, w2, w3 = _split3(w)
    logits = (_dot(h1, w1) + (_dot(h1, w2) + _dot(h2, w1))
              + (_dot(h1, w3) + _dot(h2, w2) + _dot(h3, w1))) + br_ref[...]
    lane = lax.broadcasted_iota(jnp.int32, logits.shape, 1)
    logits = jnp.where(lane < N_EXPERTS, logits, -jnp.inf)
    m1 = jnp.max(logits, axis=-1, keepdims=True)
    i1 = jnp.min(jnp.where(logits == m1, lane, SM_W), axis=-1, keepdims=True)
    rest = jnp.where(lane == i1, -jnp.inf, logits)
    m2 = jnp.max(rest, axis=-1, keepdims=True)
    i2 = jnp.min(jnp.where(rest == m2, lane, SM_W), axis=-1, keepdims=True)
    e = jnp.exp(m2 - m1)
    p1 = 1.0 / (1.0 + e)
    p2 = e / (1.0 + e)
    r_ref[...] = jnp.where(lane == 0, i1.astype(F32),
                           jnp.where(lane == 1, i2.astype(F32),
                                     jnp.where(lane == 2, p1, jnp.where(lane == 3, p2, 0.0))))


def _router(x, g, w_router, b_router, *, tm=512):
    t, d = x.shape
    return pl.pallas_call(
        _router_kernel,
        grid=(t // tm,),
        in_specs=[
            pl.BlockSpec((tm, d), lambda i: (i, 0)),
            pl.BlockSpec((1, d), lambda i: (0, 0)),
            pl.BlockSpec((d, SM_W), lambda i: (0, 0)),
            pl.BlockSpec((1, SM_W), lambda i: (0, 0)),
        ],
        out_specs=[
            pl.BlockSpec((tm, d), lambda i: (i, 0)),
            pl.BlockSpec((tm, SM_W), lambda i: (i, 0)),
        ],
        out_shape=[
            jax.ShapeDtypeStruct((t, d), F32),
            jax.ShapeDtypeStruct((t, SM_W), F32),
        ],
        compiler_params=_cparams("parallel"),
        name="router",
    )(x, g, w_router, b_router)


def _row_copy(src_ref, dst_ref, sem, src_row, dst_row):
    return pltpu.make_async_copy(src_ref.at[pl.ds(src_row, 1), :], dst_ref.at[pl.ds(dst_row, 1), :], sem)


def _gather_kernel(idx_ref, src_ref, o_ref, buf_ref, sem):
    rows = buf_ref.shape[0]
    base = pl.program_id(0) * rows

    def start(r, carry):
        _row_copy(src_ref, buf_ref, sem, idx_ref[base + r], r).start()
        return carry

    lax.fori_loop(0, rows, start, 0)

    def wait(r, carry):
        _row_copy(src_ref, buf_ref, sem, 0, r).wait()
        return carry

    lax.fori_loop(0, rows, wait, 0)
    o_ref[...] = buf_ref[...].astype(o_ref.dtype)


def _gather_rows(src, idx, *, rows=256):
    p = idx.shape[0]
    d = src.shape[1]
    return pl.pallas_call(
        _gather_kernel,
        grid_spec=pltpu.PrefetchScalarGridSpec(
            num_scalar_prefetch=1,
            grid=(p // rows,),
            in_specs=[pl.BlockSpec(memory_space=pl.ANY)],
            out_specs=pl.BlockSpec((rows, d), lambda i, idx_ref: (i, 0)),
            scratch_shapes=[pltpu.VMEM((rows, d), src.dtype), pltpu.SemaphoreType.DMA(())],
        ),
        out_shape=jax.ShapeDtypeStruct((p, d), BF16),
        compiler_params=_cparams("arbitrary"),
        name="moe_gather",
    )(idx, src)


def _expert_changed(te_ref):
    i = pl.program_id(1)
    return jnp.logical_or(i == 0, te_ref[i] != te_ref[jnp.maximum(i - 1, 0)])


def _moe_gu_kernel(te_ref, nu_ref, a_ref, wg_ref, wu_ref, o_ref, wgs_ref, wus_ref):
    used = pl.program_id(1) < nu_ref[0]

    @pl.when(_expert_changed(te_ref))
    def _():
        _cast_weight(wg_ref, wgs_ref)
        _cast_weight(wu_ref, wus_ref)

    @pl.when(used)
    def _():
        a = a_ref[...]
        o_ref[...] = (_silu(_dot(a, wgs_ref[...])) * _dot(a, wus_ref[...])).astype(o_ref.dtype)

    @pl.when(jnp.logical_not(used))
    def _():
        o_ref[...] = jnp.zeros_like(o_ref)


def _moe_gu(a, w_gu, layer, tile_expert, n_used, *, tm, tf=1024):
    p, d = a.shape
    f = w_gu.shape[3] // 2
    nf = f // tf
    return pl.pallas_call(
        _moe_gu_kernel,
        grid_spec=pltpu.PrefetchScalarGridSpec(
            num_scalar_prefetch=2,
            grid=(nf, p // tm),
            in_specs=[
                pl.BlockSpec((tm, d), lambda j, i, te, nu: (i, 0)),
                pl.BlockSpec((1, 1, d, tf), lambda j, i, te, nu: (layer, te[i], 0, j)),
                pl.BlockSpec((1, 1, d, tf), lambda j, i, te, nu: (layer, te[i], 0, nf + j)),
            ],
            out_specs=pl.BlockSpec((tm, tf), lambda j, i, te, nu: (i, j)),
            scratch_shapes=[pltpu.VMEM((d, tf), BF16), pltpu.VMEM((d, tf), BF16)],
        ),
        out_shape=jax.ShapeDtypeStruct((p, f), BF16),
        compiler_params=_cparams("parallel", "arbitrary"),
        name="moe_gu",
    )(tile_expert, n_used, a, w_gu, w_gu)


def _moe_down_kernel(te_ref, nu_ref, a_ref, w_ref, o_ref, ws_ref):
    used = pl.program_id(1) < nu_ref[0]

    @pl.when(_expert_changed(te_ref))
    def _():
        _cast_weight(w_ref, ws_ref)

    @pl.when(used)
    def _():
        o_ref[...] = _dot(a_ref[...], ws_ref[...])

    @pl.when(jnp.logical_not(used))
    def _():
        o_ref[...] = jnp.zeros_like(o_ref)


def _moe_down(a, w_down, layer, tile_expert, n_used, *, tm, tn=512):
    p, f = a.shape
    n = w_down.shape[3]
    return pl.pallas_call(
        _moe_down_kernel,
        grid_spec=pltpu.PrefetchScalarGridSpec(
            num_scalar_prefetch=2,
            grid=(n // tn, p // tm),
            in_specs=[
                pl.BlockSpec((tm, f), lambda j, i, te, nu: (i, 0)),
                pl.BlockSpec((1, 1, f, tn), lambda j, i, te, nu: (layer, te[i], 0, j)),
            ],
            out_specs=pl.BlockSpec((tm, tn), lambda j, i, te, nu: (i, j)),
            scratch_shapes=[pltpu.VMEM((f, tn), BF16)],
        ),
        out_shape=jax.ShapeDtypeStruct((p, n), F32),
        compiler_params=_cparams("parallel", "arbitrary"),
        name="moe_down",
    )(tile_expert, n_used, a, w_down)


def _combine_kernel(p1_ref, p2_ref, y_ref, x_ref, r_ref, o_ref, b1_ref, b2_ref, sem):
    rows = b1_ref.shape[0]
    base = pl.program_id(0) * rows

    def start(r, carry):
        _row_copy(y_ref, b1_ref, sem.at[0], p1_ref[base + r], r).start()
        _row_copy(y_ref, b2_ref, sem.at[1], p2_ref[base + r], r).start()
        return carry

    lax.fori_loop(0, rows, start, 0)

    def wait(r, carry):
        _row_copy(y_ref, b1_ref, sem.at[0], 0, r).wait()
        _row_copy(y_ref, b2_ref, sem.at[1], 0, r).wait()
        return carry

    lax.fori_loop(0, rows, wait, 0)
    route = r_ref[...]
    o_ref[...] = x_ref[...] + route[:, 2:3] * b1_ref[...] + route[:, 3:4] * b2_ref[...]


def _combine(y_sorted, x, route, pos1, pos2, *, rows=256):
    t, d = x.shape
    return pl.pallas_call(
        _combine_kernel,
        grid_spec=pltpu.PrefetchScalarGridSpec(
            num_scalar_prefetch=2,
            grid=(t // rows,),
            in_specs=[
                pl.BlockSpec(memory_space=pl.ANY),
                pl.BlockSpec((rows, d), lambda i, p1, p2: (i, 0)),
                pl.BlockSpec((rows, SM_W), lambda i, p1, p2: (i, 0)),
            ],
            out_specs=pl.BlockSpec((rows, d), lambda i, p1, p2: (i, 0)),
            scratch_shapes=[pltpu.VMEM((rows, d), F32), pltpu.VMEM((rows, d), F32),
                            pltpu.SemaphoreType.DMA((2,))],
        ),
        out_shape=jax.ShapeDtypeStruct((t, d), F32),
        compiler_params=_cparams("arbitrary"),
        name="moe_combine",
    )(pos1, pos2, y_sorted, x, route)


MOE_TM = 512


def _moe_plan(route, tm):
    t = route.shape[0]
    n_tiles = (t * TOP_K) // tm + N_EXPERTS
    eid = route[:, :TOP_K].astype(jnp.int32).reshape(-1)
    onehot = (eid[:, None] == jnp.arange(N_EXPERTS, dtype=jnp.int32)[None, :]).astype(jnp.int32)
    csum = jnp.cumsum(onehot, axis=0)
    rank = jnp.sum((csum - onehot) * onehot, axis=1)
    counts = csum[-1]
    tiles_per = (counts + tm - 1) // tm
    tile_end = jnp.cumsum(tiles_per)
    start = (tile_end - tiles_per) * tm
    pos = (jnp.sum(onehot * start[None, :], axis=1) + rank).astype(jnp.int32)
    n_used = tile_end[-1].astype(jnp.int32)
    tile_id = jnp.arange(n_tiles, dtype=jnp.int32)
    tile_expert = jnp.sum((tile_id[:, None] >= tile_end[None, :]).astype(jnp.int32), axis=1)
    tile_expert = jnp.minimum(tile_expert, N_EXPERTS - 1).astype(jnp.int32)
    last_expert = jnp.max(jnp.where(counts > 0, jnp.arange(N_EXPERTS, dtype=jnp.int32), 0))
    tile_expert = jnp.where(tile_id < n_used, tile_expert, last_expert).astype(jnp.int32)
    token = jnp.repeat(jnp.arange(t, dtype=jnp.int32), TOP_K)
    row_token = jnp.zeros((n_tiles * tm,), jnp.int32).at[pos].set(token)
    pos = pos.reshape(t, TOP_K)
    return row_token, tile_expert, n_used.reshape(1), pos[:, 0], pos[:, 1]


def _moe_ffn(x, g, w_router, b_router, w_gu, w_down, layer):
    h, route = _router(x, g, w_router, b_router)
    row_token, tile_expert, n_used, pos1, pos2 = _moe_plan(route, MOE_TM)
    a = _gather_rows(h, row_token)
    act = _moe_gu(a, w_gu, layer, tile_expert, n_used, tm=MOE_TM)
    y = _moe_down(act, w_down, layer, tile_expert, n_used, tm=MOE_TM)
    return _combine(y, x, route, pos1, pos2)


def _final_norm_kernel(x_ref, g_ref, o_ref):
    o_ref[...] = _rmsnorm_rows(x_ref[...], g_ref[...])


def _final_norm(x, g, *, tm=512):
    t, d = x.shape
    return pl.pallas_call(
        _final_norm_kernel,
        grid=(t // tm,),
        in_specs=[pl.BlockSpec((tm, d), lambda i: (i, 0)), pl.BlockSpec((1, d), lambda i: (0, 0))],
        out_specs=pl.BlockSpec((tm, d), lambda i: (i, 0)),
        out_shape=jax.ShapeDtypeStruct((t, d), F32),
        compiler_params=_cparams("parallel"),
        name="final_norm",
    )(x, g)


def _lane_row(vec, offset):
    return jnp.zeros((1, SM_W), F32).at[0, offset:offset + vec.shape[0]].set(vec.astype(F32))


def _mixer(x, l, bsz, seq, p, consts):
    tril, triu, expand = consts
    h, small, small_t = _norm_small(x, p["norm_mix_g"][l][None, :], p["w_in"], l)
    proj = _proj(h, p["w_in"], l)

    if_bias = p["if_bias"][l]
    y_a = _mlstm(proj, small, small_t, p["qk_conv_w"][l], p["qk_conv_b"][l][None, :],
                 _lane_row(if_bias, 0),
                 jnp.broadcast_to(if_bias.astype(F32)[:, None], (2 * ML_HEADS, SEQ_CHUNK)),
                 p["mh_norm_g"][l][None, :], tril, triu, bsz=bsz, seq=seq)
    y_b = _ssd(proj, small, small_t, p["ssd_conv_w"][l], p["ssd_conv_b"][l][None, :],
               _lane_row(p["dt_bias"][l], SM_DT),
               jnp.broadcast_to(p["dt_bias"][l].astype(F32)[:, None], (SSD_HEADS, SEQ_CHUNK)),
               _lane_row(p["a_log"][l], SM_DT),
               jnp.broadcast_to(p["a_log"][l].astype(F32)[:, None], (SSD_HEADS, SEQ_CHUNK)),
               jnp.repeat(p["d_skip"][l].astype(F32), SSD_P)[None, :],
               p["ssd_norm_g"][l][None, :], tril, triu, expand, bsz=bsz, seq=seq)
    merged = _merge(y_a, y_b, p["w_branch_a"], p["w_branch_b"], l, proj, p["gate_bias"][l][None, :])
    return _matmul_resid(merged, p["w_out"], l, x, tm=1024, tn=512, name="out_proj")


def kernel(x, norm_mix_g, w_in, if_bias, qk_conv_w, qk_conv_b, mh_norm_g, ssd_conv_w, ssd_conv_b,
           dt_bias, a_log, d_skip, ssd_norm_g, gate_bias, w_branch_a, w_branch_b, w_out, norm_ffn_g,
           ffn_w_gu, ffn_w_down, router_w, router_b, exp_w_gu, exp_w_down, norm_final_g):
    p = dict(norm_mix_g=norm_mix_g, w_in=w_in, if_bias=if_bias, qk_conv_w=qk_conv_w,
             qk_conv_b=qk_conv_b, mh_norm_g=mh_norm_g, ssd_conv_w=ssd_conv_w, ssd_conv_b=ssd_conv_b,
             dt_bias=dt_bias, a_log=a_log, d_skip=d_skip, ssd_norm_g=ssd_norm_g, gate_bias=gate_bias,
             w_branch_a=w_branch_a, w_branch_b=w_branch_b, w_out=w_out)
    bsz, seq, d = x.shape
    x = x.reshape(bsz * seq, d)

    idx = jnp.arange(SEQ_CHUNK, dtype=jnp.int32)
    tril = (idx[:, None] >= idx[None, :]).astype(BF16)
    triu = (idx[:, None] <= idx[None, :]).astype(BF16)
    lane = jnp.arange(SM_W, dtype=jnp.int32)[:, None]
    chan = jnp.arange(SSD_WIDTH, dtype=jnp.int32)[None, :]
    expand = (lane == SM_DT + chan // SSD_P).astype(BF16)
    consts = (tril, triu, expand)

    for l in range(DEPTH):
        x = _mixer(x, l, bsz, seq, p, consts)
        g = norm_ffn_g[l][None, :]
        if l % 2 == 0:
            act = _ffn_gu(x, g, ffn_w_gu, l // 2)
            x = _matmul_resid(act, ffn_w_down, l // 2, x, tm=512, tn=512, name="ffn_down")
        else:
            w_r = jnp.pad(router_w[l // 2], ((0, 0), (0, SM_W - N_EXPERTS)))
            b_r = _lane_row(router_b[l // 2], 0)
            x = _moe_ffn(x, g, w_r, b_r, exp_w_gu, exp_w_down, l // 2)
    return _final_norm(x, norm_final_g[None, :]).reshape(bsz, seq, d)
```

```python
import functools
import math

import jax
import jax.numpy as jnp
from jax import lax
from jax.experimental import pallas as pl
from jax.experimental.pallas import tpu as pltpu

F32 = jnp.float32
BF16 = jnp.bfloat16

D_MODEL = 2048
DEPTH = 4
EPS = 1e-6
CONV_K = 4
ML_HEADS = 4
ML_DV = 512
ML_DK = 256
ML_QK = ML_HEADS * ML_DK
ML_WIDTH = ML_HEADS * ML_DV
SSD_HEADS = 32
SSD_P = 64
SSD_GROUPS = 8
SSD_N = 128
SSD_R = SSD_HEADS // SSD_GROUPS
SSD_WIDTH = SSD_HEADS * SSD_P
SSD_GN = SSD_GROUPS * SSD_N
SSD_CONV = SSD_WIDTH + 2 * SSD_GN
SSD_GW = SSD_R * SSD_P
FFN_DENSE = 5632
N_EXPERTS = 8
TOP_K = 2
FFN_EXPERT = 4096

PC_Q = 0
PC_K = 1024
PC_V = 2048
PC_O = 4096
PC_Z = 6144
PC_XBC = 8192
PC_GATE = 12288
PC_TOTAL = 16384
SRC_IF = 6144
SRC_DT = 12296
SHIFT_ZXBC = 8
SHIFT_GATE = 40
PROJ_TN = 1024
SM_W = 128
SM_COLS = 2 * SM_W
SM_DT = SRC_DT - (SRC_DT // SM_W) * SM_W

V7X_VMEM_BYTES = 64 * 1024 * 1024
VMEM_LIMIT = 56 * 1024 * 1024

MOE_DOWN_TN = 1024
SEQ_CHUNK = 256
CARRY = 8


def _cparams(*sem):
    return pltpu.CompilerParams(dimension_semantics=sem, vmem_limit_bytes=VMEM_LIMIT)


def _sigmoid(x):
    return 1.0 / (1.0 + jnp.exp(-x))


def _silu(x):
    return x * _sigmoid(x)


def _softplus(x):
    return jnp.maximum(x, 0.0) + jnp.log(1.0 + jnp.exp(-jnp.abs(x)))


def _log_sigmoid(x):
    return jnp.minimum(x, 0.0) - jnp.log(1.0 + jnp.exp(-jnp.abs(x)))


def _split3(x):
    x1 = x.astype(BF16)
    r1 = x - x1.astype(F32)
    x2 = r1.astype(BF16)
    x3 = (r1 - x2.astype(F32)).astype(BF16)
    return x1, x2, x3


def _dot(a, b):
    return jnp.dot(a, b, preferred_element_type=F32)


def _dot_nt(a, b):
    return lax.dot_general(a, b, (((1,), (1,)), ((), ())), preferred_element_type=F32)


def _dot_tn(a, b):
    return lax.dot_general(a, b, (((0,), (0,)), ((), ())), preferred_element_type=F32)


def _dot_f32_left(x, m01):
    x1, x2, x3 = _split3(x)
    return _dot(x1, m01) + _dot(x2, m01) + _dot(x3, m01)


def _dot_f32_right(m01, x):
    x1, x2, x3 = _split3(x)
    return _dot(m01, x1) + _dot(m01, x2) + _dot(m01, x3)


def _rmsnorm_rows(x, g):
    ms = jnp.mean(x * x, axis=-1, keepdims=True)
    return (x * lax.rsqrt(ms + EPS)) * g


def _norm_small_kernel(x_ref, g_ref, w0_ref, w1_ref, h_ref, s_ref, st_ref):
    hb = _rmsnorm_rows(x_ref[...], g_ref[...]).astype(BF16)
    h_ref[...] = hb
    wt = jnp.concatenate([w0_ref[0], w1_ref[0]], axis=0).astype(BF16)
    s_ref[...] = _dot_nt(hb, wt)
    st_ref[...] = _dot_nt(wt, hb)


def _norm_small(x, g, w_in_t, layer, *, tm=1024):
    t, d = x.shape
    return pl.pallas_call(
        _norm_small_kernel,
        grid=(t // tm,),
        in_specs=[
            pl.BlockSpec((tm, d), lambda i: (i, 0)),
            pl.BlockSpec((1, d), lambda i: (0, 0)),
            pl.BlockSpec((1, SM_W, d), lambda i: (layer, SRC_IF // SM_W, 0)),
            pl.BlockSpec((1, SM_W, d), lambda i: (layer, SRC_DT // SM_W, 0)),
        ],
        out_specs=[
            pl.BlockSpec((tm, d), lambda i: (i, 0)),
            pl.BlockSpec((tm, SM_COLS), lambda i: (i, 0)),
            pl.BlockSpec((SM_COLS, tm), lambda i: (0, i)),
        ],
        out_shape=[
            jax.ShapeDtypeStruct((t, d), BF16),
            jax.ShapeDtypeStruct((t, SM_COLS), F32),
            jax.ShapeDtypeStruct((SM_COLS, t), F32),
        ],
        compiler_params=_cparams("parallel"),
        name="norm_small",
    )(x, g, w_in_t, w_in_t)


CAST_ROWS = 256


def _proj_kernel(h_ref, wt_ref, o_ref, ws_ref):
    @pl.when(pl.program_id(1) == 0)
    def _():
        for r in range(0, ws_ref.shape[1], CAST_ROWS):
            ws_ref[:, r:r + CAST_ROWS] = wt_ref[0, r:r + CAST_ROWS, :].T.astype(BF16)

    o_ref[...] = _dot(h_ref[...], ws_ref[...]).astype(o_ref.dtype)


def _proj_src_row(j, tn):
    past_z = (j >= PC_Z // tn).astype(jnp.int32)
    past_gate = (j >= PC_GATE // tn).astype(jnp.int32)
    shift8 = past_z * (SHIFT_ZXBC // 8) + past_gate * ((SHIFT_GATE - SHIFT_ZXBC) // 8)
    return pl.multiple_of(j * tn + 8 * shift8, 8)


def _proj(h, w_in_t, layer, *, tm=2048):
    t, d = h.shape
    tn = PROJ_TN
    assert t % tm == 0, (t, tm)
    return pl.pallas_call(
        _proj_kernel,
        grid=(PC_TOTAL // tn, t // tm),
        in_specs=[
            pl.BlockSpec((tm, d), lambda j, i: (i, 0)),
            pl.BlockSpec((pl.Element(1), pl.Element(tn), pl.Element(d)),
                         lambda j, i: (layer, _proj_src_row(j, tn), 0)),
        ],
        out_specs=pl.BlockSpec((tm, tn), lambda j, i: (i, j)),
        out_shape=jax.ShapeDtypeStruct((t, PC_TOTAL), BF16),
        scratch_shapes=[pltpu.VMEM((d, tn), BF16)],
        compiler_params=_cparams("parallel", "arbitrary"),
        name="proj",
    )(h, w_in_t)


def _causal_conv(raw, xc_ref, carry_ref, w_ref, b_ref, first_chunk):
    length = raw.shape[0]

    @pl.when(first_chunk)
    def _():
        carry_ref[...] = jnp.zeros_like(carry_ref)

    xc_ref[0:CARRY, :] = carry_ref[...]
    xc_ref[CARRY:CARRY + length, :] = raw
    carry_ref[...] = raw[length - CARRY:, :]
    y = b_ref[...] + w_ref[CONV_K - 1:CONV_K, :] * raw
    for back in range(1, CONV_K):
        tap = xc_ref[CARRY - back:CARRY - back + length, :]
        y = y + w_ref[CONV_K - 1 - back:CONV_K - back, :] * tap
    return y


def _mlstm_kernel(q_ref, k_ref, v_ref, o_ref, sm_ref, smt_ref, cw_ref, cb_ref, ifr_ref, ifc_ref,
                  g_ref, tril_ref, triu_ref, y_ref, xc_ref, carry_ref, c_ref, n_ref, m_ref):
    length = q_ref.shape[0]
    first = pl.program_id(1) == 0

    @pl.when(first)
    def _():
        c_ref[...] = jnp.zeros_like(c_ref)
        n_ref[...] = jnp.zeros_like(n_ref)
        m_ref[...] = jnp.zeros_like(m_ref)

    raw = jnp.concatenate([q_ref[...], k_ref[...]], axis=-1).astype(F32)
    qk = _silu(_causal_conv(raw, xc_ref, carry_ref, cw_ref, cb_ref, first))

    cols = sm_ref[...] + ifr_ref[...]
    rows = smt_ref[...] + ifc_ref[...]
    b_cols = _dot_f32_right(tril_ref[...], _log_sigmoid(cols))
    b_rows = _dot_f32_left(_log_sigmoid(rows), triu_ref[...])
    r_i = lax.broadcasted_iota(jnp.int32, (length, length), 0)
    c_i = lax.broadcasted_iota(jnp.int32, (length, length), 1)
    causal = r_i >= c_i

    for h in range(ML_HEADS):
        q_h = qk[:, h * ML_DK:(h + 1) * ML_DK] * (ML_DK ** -0.5)
        k_h = qk[:, ML_QK + h * ML_DK:ML_QK + (h + 1) * ML_DK]
        qb = q_h.astype(BF16)
        kb = k_h.astype(BF16)
        v_h = v_ref[:, h * ML_DV:(h + 1) * ML_DV]
        i_col = cols[:, h:h + 1]
        b_col = b_cols[:, ML_HEADS + h:ML_HEADS + h + 1]
        i_row = rows[h:h + 1, :]
        b_row = b_rows[ML_HEADS + h:ML_HEADS + h + 1, :]
        m_prev = m_ref[h:h + 1, 0:1]
        c_prev = c_ref[h]
        n_prev = n_ref[h:h + 1, :]

        log_d = jnp.where(causal, b_col - b_row + i_row, -jnp.inf)
        m_inter = b_col + m_prev
        m_t = jnp.maximum(m_inter, jnp.max(log_d, axis=-1, keepdims=True))
        scores = _dot_nt(qb, kb) * jnp.exp(log_d - m_t)
        inter = jnp.exp(m_inter - m_t)
        num = _dot(scores.astype(BF16), v_h) + inter * _dot(qb, c_prev.astype(BF16))
        den = (jnp.sum(scores, axis=-1, keepdims=True)
               + inter * jnp.sum(q_h * n_prev, axis=-1, keepdims=True))
        hh = num / jnp.maximum(jnp.abs(den), jnp.exp(-m_t))
        hh = hh * lax.rsqrt(jnp.mean(hh * hh, axis=-1, keepdims=True) + EPS)
        gate = _sigmoid(o_ref[:, h * ML_DV:(h + 1) * ML_DV].astype(F32))
        y_ref[:, h * ML_DV:(h + 1) * ML_DV] = (
            hh * g_ref[:, h * ML_DV:(h + 1) * ML_DV] * gate).astype(y_ref.dtype)

        b_last = b_col[length - 1:length, :]
        log_w_row = b_last - b_row + i_row
        m_new = jnp.maximum(b_last + m_prev, jnp.max(log_w_row, axis=-1, keepdims=True))
        w_col = jnp.exp(b_last - b_col + i_col - m_new)
        decay = jnp.exp(b_last + m_prev - m_new)
        vw = (v_h.astype(F32) * w_col).astype(BF16)
        c_ref[h] = decay * c_prev + _dot_tn(kb, vw)
        n_ref[h:h + 1, :] = decay * n_prev + jnp.sum(k_h * w_col, axis=0, keepdims=True)
        m_ref[h:h + 1, :] = jnp.broadcast_to(m_new, (1, m_ref.shape[1]))


def _mlstm(proj, small, small_t, conv_w, conv_b, if_row, if_col, norm_g, tril, triu, *, bsz, seq):
    length = SEQ_CHUNK
    nc = seq // length
    t = bsz * seq
    row = lambda b, c: b * nc + c
    return pl.pallas_call(
        _mlstm_kernel,
        grid=(bsz, nc),
        in_specs=[
            pl.BlockSpec((length, ML_QK), lambda b, c: (row(b, c), PC_Q // ML_QK)),
            pl.BlockSpec((length, ML_QK), lambda b, c: (row(b, c), PC_K // ML_QK)),
            pl.BlockSpec((length, ML_WIDTH), lambda b, c: (row(b, c), PC_V // ML_WIDTH)),
            pl.BlockSpec((length, ML_WIDTH), lambda b, c: (row(b, c), PC_O // ML_WIDTH)),
            pl.BlockSpec((length, SM_W), lambda b, c: (row(b, c), 0)),
            pl.BlockSpec((2 * ML_HEADS, length), lambda b, c: (0, row(b, c))),
            pl.BlockSpec((CONV_K, 2 * ML_QK), lambda b, c: (0, 0)),
            pl.BlockSpec((1, 2 * ML_QK), lambda b, c: (0, 0)),
            pl.BlockSpec((1, SM_W), lambda b, c: (0, 0)),
            pl.BlockSpec((2 * ML_HEADS, length), lambda b, c: (0, 0)),
            pl.BlockSpec((1, ML_WIDTH), lambda b, c: (0, 0)),
            pl.BlockSpec((length, length), lambda b, c: (0, 0)),
            pl.BlockSpec((length, length), lambda b, c: (0, 0)),
        ],
        out_specs=pl.BlockSpec((length, ML_WIDTH), lambda b, c: (row(b, c), 0)),
        out_shape=jax.ShapeDtypeStruct((t, ML_WIDTH), BF16),
        scratch_shapes=[
            pltpu.VMEM((CARRY + length, 2 * ML_QK), F32),
            pltpu.VMEM((CARRY, 2 * ML_QK), F32),
            pltpu.VMEM((ML_HEADS, ML_DK, ML_DV), F32),
            pltpu.VMEM((2 * ML_HEADS, ML_DK), F32),
            pltpu.VMEM((2 * ML_HEADS, 128), F32),
        ],
        compiler_params=_cparams("parallel", "arbitrary"),
        name="mlstm",
    )(proj, proj, proj, proj, small, small_t, conv_w, conv_b, if_row, if_col, norm_g, tril, triu)


def _ssd_kernel(xbc_ref, z_ref, sm_ref, smt_ref, cw_ref, cb_ref, dtb_r_ref, dtb_c_ref, al_r_ref,
                al_c_ref, dsk_ref, g_ref, tril_ref, triu_ref, e_ref, y_ref,
                xc_ref, carry_ref, st_ref):
    length = xbc_ref.shape[0]
    first = pl.program_id(1) == 0

    @pl.when(first)
    def _():
        st_ref[...] = jnp.zeros_like(st_ref)

    xconv = _silu(_causal_conv(xbc_ref[...].astype(F32), xc_ref, carry_ref, cw_ref, cb_ref, first))

    lane = lax.broadcasted_iota(jnp.int32, (1, SM_W), 1)
    head_lane = jnp.logical_and(lane >= SM_DT, lane < SM_DT + SSD_HEADS)
    dt_c = jnp.where(head_lane, _softplus(sm_ref[...] + dtb_r_ref[...]), 0.0)
    dt_r = _softplus(smt_ref[SM_DT:SM_DT + SSD_HEADS, :] + dtb_c_ref[...])
    a_c = dt_c * (-jnp.exp(al_r_ref[...]))
    a_r = dt_r * (-jnp.exp(al_c_ref[...]))
    acum_c = _dot_f32_right(tril_ref[...], a_c)
    acum_r = _dot_f32_left(a_r, triu_ref[...])
    a_last = acum_c[length - 1:length, :]

    e01 = e_ref[...]
    dt_e = _dot_f32_left(dt_c, e01)
    from_start_e = _dot_f32_left(jnp.exp(acum_c), e01)
    to_end_e = _dot_f32_left(jnp.exp(a_last - acum_c), e01)
    chunk_e = _dot_f32_left(jnp.broadcast_to(jnp.exp(a_last), (8, SM_W)), e01)[0:1, :]

    r_i = lax.broadcasted_iota(jnp.int32, (length, length), 0)
    c_i = lax.broadcasted_iota(jnp.int32, (length, length), 1)
    causal = r_i >= c_i

    for g in range(SSD_GROUPS):
        ch = slice(g * SSD_GW, (g + 1) * SSD_GW)
        xs_g = xconv[:, ch]
        bm = xconv[:, SSD_WIDTH + g * SSD_N:SSD_WIDTH + (g + 1) * SSD_N].astype(BF16)
        cm = xconv[:, SSD_WIDTH + SSD_GN + g * SSD_N:SSD_WIDTH + SSD_GN + (g + 1) * SSD_N].astype(BF16)
        xdt = xs_g * dt_e[:, ch]
        cb = _dot_nt(cm, bm)
        prev = st_ref[g]
        y = _dot(cm, prev.astype(BF16)) * from_start_e[:, ch] + dsk_ref[:, ch] * xs_g
        diag = []
        for r in range(SSD_R):
            h = g * SSD_R + r
            seg = acum_c[:, SM_DT + h:SM_DT + h + 1] - acum_r[h:h + 1, :]
            att = cb * jnp.exp(jnp.where(causal, seg, -jnp.inf))
            diag.append(_dot(att.astype(BF16), xdt[:, r * SSD_P:(r + 1) * SSD_P].astype(BF16)))
        y = y + jnp.concatenate(diag, axis=-1)
        st_ref[g] = prev * chunk_e[:, ch] + _dot_tn(bm, (xdt * to_end_e[:, ch]).astype(BF16))
        y = y * _silu(z_ref[:, ch].astype(F32))
        y = y * lax.rsqrt(jnp.mean(y * y, axis=-1, keepdims=True) + EPS)
        y_ref[:, ch] = (y * g_ref[:, ch]).astype(y_ref.dtype)


def _ssd(proj, small, small_t, conv_w, conv_b, dtb_row, dtb_col, al_row, al_col, dskip_e, norm_g,
         tril, triu, expand, *, bsz, seq):
    length = SEQ_CHUNK
    nc = seq // length
    t = bsz * seq
    row = lambda b, c: b * nc + c
    const = lambda b, c: (0, 0)
    return pl.pallas_call(
        _ssd_kernel,
        grid=(bsz, nc),
        in_specs=[
            pl.BlockSpec((length, SSD_CONV), lambda b, c: (row(b, c), PC_XBC // SSD_CONV)),
            pl.BlockSpec((length, SSD_WIDTH), lambda b, c: (row(b, c), PC_Z // SSD_WIDTH)),
            pl.BlockSpec((length, SM_W), lambda b, c: (row(b, c), 1)),
            pl.BlockSpec((2 * SSD_HEADS, length), lambda b, c: (SM_W // (2 * SSD_HEADS), row(b, c))),
            pl.BlockSpec((CONV_K, SSD_CONV), const),
            pl.BlockSpec((1, SSD_CONV), const),
            pl.BlockSpec((1, SM_W), const),
            pl.BlockSpec((SSD_HEADS, length), const),
            pl.BlockSpec((1, SM_W), const),
            pl.BlockSpec((SSD_HEADS, length), const),
            pl.BlockSpec((1, SSD_WIDTH), const),
            pl.BlockSpec((1, SSD_WIDTH), const),
            pl.BlockSpec((length, length), const),
            pl.BlockSpec((length, length), const),
            pl.BlockSpec((SM_W, SSD_WIDTH), const),
        ],
        out_specs=pl.BlockSpec((length, SSD_WIDTH), lambda b, c: (row(b, c), 0)),
        out_shape=jax.ShapeDtypeStruct((t, SSD_WIDTH), BF16),
        scratch_shapes=[
            pltpu.VMEM((CARRY + length, SSD_CONV), F32),
            pltpu.VMEM((CARRY, SSD_CONV), F32),
            pltpu.VMEM((SSD_GROUPS, SSD_N, SSD_GW), F32),
        ],
        compiler_params=_cparams("parallel", "arbitrary"),
        name="ssd",
    )(proj, proj, small, small_t, conv_w, conv_b, dtb_row, dtb_col, al_row, al_col, dskip_e, norm_g,
      tril, triu, expand)


def _cast_weight(w_ref, ws_ref):
    lead = (0,) * (len(w_ref.shape) - 2)
    for r in range(0, ws_ref.shape[0], CAST_ROWS):
        ws_ref[r:r + CAST_ROWS, :] = w_ref[lead + (slice(r, r + CAST_ROWS), slice(None))].astype(BF16)


def _merge_kernel(ya_ref, yb_ref, wa_ref, wb_ref, ga_ref, gb_ref, ba_ref, bb_ref, o_ref, was_ref, wbs_ref):
    @pl.when(pl.program_id(1) == 0)
    def _():
        _cast_weight(wa_ref, was_ref)
        _cast_weight(wb_ref, wbs_ref)

    ga = _sigmoid(ga_ref[...].astype(F32) + ba_ref[...])
    gb = _sigmoid(gb_ref[...].astype(F32) + bb_ref[...])
    o_ref[...] = (ga * _dot(ya_ref[...], was_ref[...])
                  + gb * _dot(yb_ref[...], wbs_ref[...])).astype(o_ref.dtype)


def _merge(ya, yb, wa, wb, layer, proj, gate_bias, *, tm=1024, tn=512):
    t, d = ya.shape
    n = wa.shape[2]
    ga0 = PC_GATE // tn
    gb0 = (PC_GATE + n) // tn
    return pl.pallas_call(
        _merge_kernel,
        grid=(n // tn, t // tm),
        in_specs=[
            pl.BlockSpec((tm, d), lambda j, i: (i, 0)),
            pl.BlockSpec((tm, d), lambda j, i: (i, 0)),
            pl.BlockSpec((1, d, tn), lambda j, i: (layer, 0, j)),
            pl.BlockSpec((1, d, tn), lambda j, i: (layer, 0, j)),
            pl.BlockSpec((tm, tn), lambda j, i: (i, ga0 + j)),
            pl.BlockSpec((tm, tn), lambda j, i: (i, gb0 + j)),
            pl.BlockSpec((1, tn), lambda j, i: (0, j)),
            pl.BlockSpec((1, tn), lambda j, i: (0, n // tn + j)),
        ],
        out_specs=pl.BlockSpec((tm, tn), lambda j, i: (i, j)),
        out_shape=jax.ShapeDtypeStruct((t, n), BF16),
        scratch_shapes=[pltpu.VMEM((d, tn), BF16), pltpu.VMEM((d, tn), BF16)],
        compiler_params=_cparams("parallel", "arbitrary"),
        name="merge",
    )(ya, yb, wa, wb, proj, proj, gate_bias, gate_bias)


def _matmul_resid_kernel(a_ref, w_ref, r_ref, o_ref, ws_ref):
    @pl.when(pl.program_id(1) == 0)
    def _():
        _cast_weight(w_ref, ws_ref)

    o_ref[...] = r_ref[...] + _dot(a_ref[...], ws_ref[...])


def _matmul_resid(a, w, layer, resid, *, tm, tn, name):
    t, k = a.shape
    n = w.shape[2]
    return pl.pallas_call(
        _matmul_resid_kernel,
        grid=(n // tn, t // tm),
        in_specs=[
            pl.BlockSpec((tm, k), lambda j, i: (i, 0)),
            pl.BlockSpec((1, k, tn), lambda j, i: (layer, 0, j)),
            pl.BlockSpec((tm, tn), lambda j, i: (i, j)),
        ],
        out_specs=pl.BlockSpec((tm, tn), lambda j, i: (i, j)),
        out_shape=jax.ShapeDtypeStruct((t, n), F32),
        scratch_shapes=[pltpu.VMEM((k, tn), BF16)],
        compiler_params=_cparams("parallel", "arbitrary"),
        name=name,
    )(a, w, resid)


def _ffn_gu_kernel(x_ref, g_ref, wg_ref, wu_ref, o_ref, h_ref):
    @pl.when(pl.program_id(1) == 0)
    def _():
        h_ref[...] = _rmsnorm_rows(x_ref[...], g_ref[...]).astype(BF16)

    h = h_ref[...]
    gate = _dot(h, wg_ref[0].astype(BF16))
    up = _dot(h, wu_ref[0].astype(BF16))
    o_ref[...] = (_silu(gate) * up).astype(o_ref.dtype)


def _ffn_gu(x, g, w_gu, layer, *, tm=1024, tf=512):
    t, d = x.shape
    f = w_gu.shape[2] // 2
    nf = f // tf
    return pl.pallas_call(
        _ffn_gu_kernel,
        grid=(t // tm, nf),
        in_specs=[
            pl.BlockSpec((tm, d), lambda i, j: (i, 0)),
            pl.BlockSpec((1, d), lambda i, j: (0, 0)),
            pl.BlockSpec((1, d, tf), lambda i, j: (layer, 0, j)),
            pl.BlockSpec((1, d, tf), lambda i, j: (layer, 0, nf + j)),
        ],
        out_specs=pl.BlockSpec((tm, tf), lambda i, j: (i, j)),
        out_shape=jax.ShapeDtypeStruct((t, f), BF16),
        scratch_shapes=[pltpu.VMEM((tm, d), BF16)],
        compiler_params=_cparams("parallel", "arbitrary"),
        name="ffn_gu",
    )(x, g, w_gu, w_gu)


def _pack_pairs(lo, hi):
    lo_bits = lax.bitcast_convert_type(lo.astype(BF16).astype(F32), jnp.uint32)
    hi_bits = lax.bitcast_convert_type(hi.astype(BF16).astype(F32), jnp.uint32)
    return (hi_bits & jnp.uint32(0xFFFF0000)) | (lo_bits >> 16)


def _unpack_pairs(packed):
    lo = lax.bitcast_convert_type(packed << 16, F32)
    hi = lax.bitcast_convert_type(packed & jnp.uint32(0xFFFF0000), F32)
    return lo, hi


def _router_kernel(x_ref, g_ref, wr_ref, br_ref, h_ref, r_ref):
    h = _rmsnorm_rows(x_ref[...], g_ref[...])
    half = h.shape[1] // 2
    h_ref[...] = _pack_pairs(h[:, :half], h[:, half:])
    h1, h2, h3 = _split3(h)
    w = wr_ref[...]
    w1, w2, w3 = _split3(w)
    logits = (_dot(h1, w1) + (_dot(h1, w2) + _dot(h2, w1))
              + (_dot(h1, w3) + _dot(h2, w2) + _dot(h3, w1))) + br_ref[...]
    lane = lax.broadcasted_iota(jnp.int32, logits.shape, 1)
    logits = jnp.where(lane < N_EXPERTS, logits, -jnp.inf)
    m1 = jnp.max(logits, axis=-1, keepdims=True)
    i1 = jnp.min(jnp.where(logits == m1, lane, SM_W), axis=-1, keepdims=True)
    rest = jnp.where(lane == i1, -jnp.inf, logits)
    m2 = jnp.max(rest, axis=-1, keepdims=True)
    i2 = jnp.min(jnp.where(rest == m2, lane, SM_W), axis=-1, keepdims=True)
    e = jnp.exp(m2 - m1)
    p1 = 1.0 / (1.0 + e)
    p2 = e / (1.0 + e)
    r_ref[...] = jnp.where(lane == 0, i1.astype(F32),
                           jnp.where(lane == 1, i2.astype(F32),
                                     jnp.where(lane == 2, p1, jnp.where(lane == 3, p2, 0.0))))


def _router(x, g, w_router, b_router, *, tm=512):
    t, d = x.shape
    return pl.pallas_call(
        _router_kernel,
        grid=(t // tm,),
        in_specs=[
            pl.BlockSpec((tm, d), lambda i: (i, 0)),
            pl.BlockSpec((1, d), lambda i: (0, 0)),
            pl.BlockSpec((d, SM_W), lambda i: (0, 0)),
            pl.BlockSpec((1, SM_W), lambda i: (0, 0)),
        ],
        out_specs=[
            pl.BlockSpec((tm, d // 2), lambda i: (i, 0)),
            pl.BlockSpec((tm, SM_W), lambda i: (i, 0)),
        ],
        out_shape=[
            jax.ShapeDtypeStruct((t, d // 2), jnp.uint32),
            jax.ShapeDtypeStruct((t, SM_W), F32),
        ],
        compiler_params=_cparams("parallel"),
        name="router",
    )(x, g, w_router, b_router)


def _row_copy(src_ref, dst_ref, sem, src_row, dst_row):
    return pltpu.make_async_copy(src_ref.at[pl.ds(src_row, 1), :], dst_ref.at[pl.ds(dst_row, 1), :], sem)


def _gather_kernel(idx_ref, src_ref, o_ref, buf_ref, sem):
    rows = buf_ref.shape[0]
    base = pl.program_id(0) * rows

    def start(r, carry):
        _row_copy(src_ref, buf_ref, sem, idx_ref[base + r], r).start()
        return carry

    lax.fori_loop(0, rows, start, 0)

    def wait(r, carry):
        _row_copy(src_ref, buf_ref, sem, 0, r).wait()
        return carry

    lax.fori_loop(0, rows, wait, 0)
    lo, hi = _unpack_pairs(buf_ref[...])
    half = buf_ref.shape[1]
    o_ref[:, :half] = lo.astype(o_ref.dtype)
    o_ref[:, half:] = hi.astype(o_ref.dtype)


def _gather_rows(src, idx, *, rows=256):
    p = idx.shape[0]
    half = src.shape[1]
    d = 2 * half
    return pl.pallas_call(
        _gather_kernel,
        grid_spec=pltpu.PrefetchScalarGridSpec(
            num_scalar_prefetch=1,
            grid=(p // rows,),
            in_specs=[pl.BlockSpec(memory_space=pl.ANY)],
            out_specs=pl.BlockSpec((rows, d), lambda i, idx_ref: (i, 0)),
            scratch_shapes=[pltpu.VMEM((rows, half), src.dtype), pltpu.SemaphoreType.DMA(())],
        ),
        out_shape=jax.ShapeDtypeStruct((p, d), BF16),
        compiler_params=_cparams("arbitrary"),
        name="moe_gather",
    )(idx, src)


def _expert_changed(te_ref):
    i = pl.program_id(1)
    return jnp.logical_or(i == 0, te_ref[i] != te_ref[jnp.maximum(i - 1, 0)])


def _moe_gu_kernel(te_ref, nu_ref, a_ref, wg_ref, wu_ref, o_ref, wgs_ref, wus_ref):
    used = pl.program_id(1) < nu_ref[0]

    @pl.when(_expert_changed(te_ref))
    def _():
        _cast_weight(wg_ref, wgs_ref)
        _cast_weight(wu_ref, wus_ref)

    @pl.when(used)
    def _():
        a = a_ref[...]
        o_ref[...] = (_silu(_dot(a, wgs_ref[...])) * _dot(a, wus_ref[...])).astype(o_ref.dtype)

    @pl.when(jnp.logical_not(used))
    def _():
        o_ref[...] = jnp.zeros_like(o_ref)


def _moe_gu(a, w_gu, layer, tile_expert, n_used, *, tm, tf=1024):
    p, d = a.shape
    f = w_gu.shape[3] // 2
    nf = f // tf
    return pl.pallas_call(
        _moe_gu_kernel,
        grid_spec=pltpu.PrefetchScalarGridSpec(
            num_scalar_prefetch=2,
            grid=(nf, p // tm),
            in_specs=[
                pl.BlockSpec((tm, d), lambda j, i, te, nu: (i, 0)),
                pl.BlockSpec((1, 1, d, tf), lambda j, i, te, nu: (layer, te[i], 0, j)),
                pl.BlockSpec((1, 1, d, tf), lambda j, i, te, nu: (layer, te[i], 0, nf + j)),
            ],
            out_specs=pl.BlockSpec((tm, tf), lambda j, i, te, nu: (i, j)),
            scratch_shapes=[pltpu.VMEM((d, tf), BF16), pltpu.VMEM((d, tf), BF16)],
        ),
        out_shape=jax.ShapeDtypeStruct((p, f), BF16),
        compiler_params=_cparams("parallel", "arbitrary"),
        name="moe_gu",
    )(tile_expert, n_used, a, w_gu, w_gu)


def _moe_down_kernel(te_ref, nu_ref, a_ref, w_ref, o_ref, ws_ref):
    used = pl.program_id(1) < nu_ref[0]

    @pl.when(_expert_changed(te_ref))
    def _():
        _cast_weight(w_ref, ws_ref)

    @pl.when(used)
    def _():
        y = _dot(a_ref[...], ws_ref[...])
        half = y.shape[1] // 2
        o_ref[...] = _pack_pairs(y[:, :half], y[:, half:])

    @pl.when(jnp.logical_not(used))
    def _():
        o_ref[...] = jnp.zeros_like(o_ref)


def _moe_down(a, w_down, layer, tile_expert, n_used, *, tm, tn=MOE_DOWN_TN):
    p, f = a.shape
    n = w_down.shape[3]
    return pl.pallas_call(
        _moe_down_kernel,
        grid_spec=pltpu.PrefetchScalarGridSpec(
            num_scalar_prefetch=2,
            grid=(n // tn, p // tm),
            in_specs=[
                pl.BlockSpec((tm, f), lambda j, i, te, nu: (i, 0)),
                pl.BlockSpec((1, 1, f, tn), lambda j, i, te, nu: (layer, te[i], 0, j)),
            ],
            out_specs=pl.BlockSpec((tm, tn // 2), lambda j, i, te, nu: (i, j)),
            scratch_shapes=[pltpu.VMEM((f, tn), BF16)],
        ),
        out_shape=jax.ShapeDtypeStruct((p, n // 2), jnp.uint32),
        compiler_params=_cparams("parallel", "arbitrary"),
        name="moe_down",
    )(tile_expert, n_used, a, w_down)


def _combine_kernel(p1_ref, p2_ref, y_ref, x_ref, r_ref, o_ref, b1_ref, b2_ref, sem):
    rows = b1_ref.shape[0]
    base = pl.program_id(0) * rows

    def start(r, carry):
        _row_copy(y_ref, b1_ref, sem.at[0], p1_ref[base + r], r).start()
        _row_copy(y_ref, b2_ref, sem.at[1], p2_ref[base + r], r).start()
        return carry

    lax.fori_loop(0, rows, start, 0)

    def wait(r, carry):
        _row_copy(y_ref, b1_ref, sem.at[0], 0, r).wait()
        _row_copy(y_ref, b2_ref, sem.at[1], 0, r).wait()
        return carry

    lax.fori_loop(0, rows, wait, 0)
    route = r_ref[...]
    w1 = route[:, 2:3]
    w2 = route[:, 3:4]
    lo1, hi1 = _unpack_pairs(b1_ref[...])
    lo2, hi2 = _unpack_pairs(b2_ref[...])
    lo = w1 * lo1 + w2 * lo2
    hi = w1 * hi1 + w2 * hi2
    hw = MOE_DOWN_TN // 2
    for j in range(lo.shape[1] // hw):
        c = j * MOE_DOWN_TN
        o_ref[:, c:c + hw] = x_ref[:, c:c + hw] + lo[:, j * hw:(j + 1) * hw]
        o_ref[:, c + hw:c + 2 * hw] = x_ref[:, c + hw:c + 2 * hw] + hi[:, j * hw:(j + 1) * hw]


def _combine(y_sorted, x, route, pos1, pos2, *, rows=256):
    t, d = x.shape
    half = y_sorted.shape[1]
    return pl.pallas_call(
        _combine_kernel,
        grid_spec=pltpu.PrefetchScalarGridSpec(
            num_scalar_prefetch=2,
            grid=(t // rows,),
            in_specs=[
                pl.BlockSpec(memory_space=pl.ANY),
                pl.BlockSpec((rows, d), lambda i, p1, p2: (i, 0)),
                pl.BlockSpec((rows, SM_W), lambda i, p1, p2: (i, 0)),
            ],
            out_specs=pl.BlockSpec((rows, d), lambda i, p1, p2: (i, 0)),
            scratch_shapes=[pltpu.VMEM((rows, half), jnp.uint32), pltpu.VMEM((rows, half), jnp.uint32),
                            pltpu.SemaphoreType.DMA((2,))],
        ),
        out_shape=jax.ShapeDtypeStruct((t, d), F32),
        compiler_params=_cparams("arbitrary"),
        name="moe_combine",
    )(pos1, pos2, y_sorted, x, route)


MOE_TM = 512


def _moe_plan(route, tm):
    t = route.shape[0]
    n_tiles = (t * TOP_K) // tm + N_EXPERTS
    eid = route[:, :TOP_K].astype(jnp.int32).reshape(-1)
    onehot = (eid[:, None] == jnp.arange(N_EXPERTS, dtype=jnp.int32)[None, :]).astype(jnp.int32)
    csum = jnp.cumsum(onehot, axis=0)
    rank = jnp.sum((csum - onehot) * onehot, axis=1)
    counts = csum[-1]
    tiles_per = (counts + tm - 1) // tm
    tile_end = jnp.cumsum(tiles_per)
    start = (tile_end - tiles_per) * tm
    pos = (jnp.sum(onehot * start[None, :], axis=1) + rank).astype(jnp.int32)
    n_used = tile_end[-1].astype(jnp.int32)
    tile_id = jnp.arange(n_tiles, dtype=jnp.int32)
    tile_expert = jnp.sum((tile_id[:, None] >= tile_end[None, :]).astype(jnp.int32), axis=1)
    tile_expert = jnp.minimum(tile_expert, N_EXPERTS - 1).astype(jnp.int32)
    last_expert = jnp.max(jnp.where(counts > 0, jnp.arange(N_EXPERTS, dtype=jnp.int32), 0))
    tile_expert = jnp.where(tile_id < n_used, tile_expert, last_expert).astype(jnp.int32)
    token = jnp.repeat(jnp.arange(t, dtype=jnp.int32), TOP_K)
    row_token = jnp.zeros((n_tiles * tm,), jnp.int32).at[pos].set(token)
    pos = pos.reshape(t, TOP_K)
    return row_token, tile_expert, n_used.reshape(1), pos[:, 0], pos[:, 1]


def _moe_ffn(x, g, w_router, b_router, w_gu, w_down, layer):
    h, route = _router(x, g, w_router, b_router)
    row_token, tile_expert, n_used, pos1, pos2 = _moe_plan(route, MOE_TM)
    a = _gather_rows(h, row_token)
    act = _moe_gu(a, w_gu, layer, tile_expert, n_used, tm=MOE_TM)
    y = _moe_down(act, w_down, layer, tile_expert, n_used, tm=MOE_TM)
    return _combine(y, x, route, pos1, pos2)


def _final_norm_kernel(x_ref, g_ref, o_ref):
    o_ref[...] = _rmsnorm_rows(x_ref[...], g_ref[...])


def _final_norm(x, g, *, tm=512):
    t, d = x.shape
    return pl.pallas_call(
        _final_norm_kernel,
        grid=(t // tm,),
        in_specs=[pl.BlockSpec((tm, d), lambda i: (i, 0)), pl.BlockSpec((1, d), lambda i: (0, 0))],
        out_specs=pl.BlockSpec((tm, d), lambda i: (i, 0)),
        out_shape=jax.ShapeDtypeStruct((t, d), F32),
        compiler_params=_cparams("parallel"),
        name="final_norm",
    )(x, g)


def _lane_row(vec, offset):
    return jnp.zeros((1, SM_W), F32).at[0, offset:offset + vec.shape[0]].set(vec.astype(F32))


def _mixer(x, l, bsz, seq, p, consts):
    tril, triu, expand = consts
    h, small, small_t = _norm_small(x, p["norm_mix_g"][l][None, :], p["w_in_t"], l)
    proj = _proj(h, p["w_in_t"], l)

    if_bias = p["if_bias"][l]
    y_a = _mlstm(proj, small, small_t, p["qk_conv_w"][l], p["qk_conv_b"][l][None, :],
                 _lane_row(if_bias, 0),
                 jnp.broadcast_to(if_bias.astype(F32)[:, None], (2 * ML_HEADS, SEQ_CHUNK)),
                 p["mh_norm_g"][l][None, :], tril, triu, bsz=bsz, seq=seq)
    y_b = _ssd(proj, small, small_t, p["ssd_conv_w"][l], p["ssd_conv_b"][l][None, :],
               _lane_row(p["dt_bias"][l], SM_DT),
               jnp.broadcast_to(p["dt_bias"][l].astype(F32)[:, None], (SSD_HEADS, SEQ_CHUNK)),
               _lane_row(p["a_log"][l], SM_DT),
               jnp.broadcast_to(p["a_log"][l].astype(F32)[:, None], (SSD_HEADS, SEQ_CHUNK)),
               jnp.repeat(p["d_skip"][l].astype(F32), SSD_P)[None, :],
               p["ssd_norm_g"][l][None, :], tril, triu, expand, bsz=bsz, seq=seq)
    merged = _merge(y_a, y_b, p["w_branch_a"], p["w_branch_b"], l, proj, p["gate_bias"][l][None, :])
    return _matmul_resid(merged, p["w_out"], l, x, tm=1024, tn=512, name="out_proj")


def kernel(x, norm_mix_g, w_in, if_bias, qk_conv_w, qk_conv_b, mh_norm_g, ssd_conv_w, ssd_conv_b,
           dt_bias, a_log, d_skip, ssd_norm_g, gate_bias, w_branch_a, w_branch_b, w_out, norm_ffn_g,
           ffn_w_gu, ffn_w_down, router_w, router_b, exp_w_gu, exp_w_down, norm_final_g):
    p = dict(norm_mix_g=norm_mix_g, w_in_t=jnp.swapaxes(w_in, 1, 2), if_bias=if_bias, qk_conv_w=qk_conv_w,
             qk_conv_b=qk_conv_b, mh_norm_g=mh_norm_g, ssd_conv_w=ssd_conv_w, ssd_conv_b=ssd_conv_b,
             dt_bias=dt_bias, a_log=a_log, d_skip=d_skip, ssd_norm_g=ssd_norm_g, gate_bias=gate_bias,
             w_branch_a=w_branch_a, w_branch_b=w_branch_b, w_out=w_out)
    bsz, seq, d = x.shape
    assert d == D_MODEL and seq % SEQ_CHUNK == 0 and (bsz * seq) % 2048 == 0, x.shape
    x = x.reshape(bsz * seq, d)

    idx = jnp.arange(SEQ_CHUNK, dtype=jnp.int32)
    tril = (idx[:, None] >= idx[None, :]).astype(BF16)
    triu = (idx[:, None] <= idx[None, :]).astype(BF16)
    lane = jnp.arange(SM_W, dtype=jnp.int32)[:, None]
    chan = jnp.arange(SSD_WIDTH, dtype=jnp.int32)[None, :]
    expand = (lane == SM_DT + chan // SSD_P).astype(BF16)
    consts = (tril, triu, expand)

    for l in range(DEPTH):
        x = _mixer(x, l, bsz, seq, p, consts)
        g = norm_ffn_g[l][None, :]
        if l % 2 == 0:
            act = _ffn_gu(x, g, ffn_w_gu, l // 2)
            x = _matmul_resid(act, ffn_w_down, l // 2, x, tm=512, tn=512, name="ffn_down")
        else:
            w_r = jnp.pad(router_w[l // 2], ((0, 0), (0, SM_W - N_EXPERTS)))
            b_r = _lane_row(router_b[l // 2], 0)
            x = _moe_ffn(x, g, w_r, b_r, exp_w_gu, exp_w_down, l // 2)
    return _final_norm(x, norm_final_g[None, :]).reshape(bsz, seq, d)
```

```python
import functools
import math

import jax
import jax.numpy as jnp
from jax import lax
from jax.experimental import pallas as pl
from jax.experimental.pallas import tpu as pltpu

F32 = jnp.float32
BF16 = jnp.bfloat16

D_MODEL = 2048
DEPTH = 4
EPS = 1e-6
CONV_K = 4
ML_HEADS = 4
ML_DV = 512
ML_DK = 256
ML_QK = ML_HEADS * ML_DK
ML_WIDTH = ML_HEADS * ML_DV
SSD_HEADS = 32
SSD_P = 64
SSD_GROUPS = 8
SSD_N = 128
SSD_R = SSD_HEADS // SSD_GROUPS
SSD_WIDTH = SSD_HEADS * SSD_P
SSD_GN = SSD_GROUPS * SSD_N
SSD_CONV = SSD_WIDTH + 2 * SSD_GN
SSD_GW = SSD_R * SSD_P
FFN_DENSE = 5632
N_EXPERTS = 8
TOP_K = 2
FFN_EXPERT = 4096

PC_Q = 0
PC_K = 1024
PC_V = 2048
PC_O = 4096
PC_Z = 6144
PC_XBC = 8192
PC_GATE = 12288
PC_TOTAL = 16384
SRC_IF = 6144
SRC_DT = 12296
SHIFT_ZXBC = 8
SHIFT_GATE = 40
PROJ_TN = 1024
SM_W = 128
SM_COLS = 2 * SM_W
SM_DT = SRC_DT - (SRC_DT // SM_W) * SM_W

V7X_VMEM_BYTES = 64 * 1024 * 1024
VMEM_LIMIT = 58 * 1024 * 1024

MOE_DOWN_TN = 1024
SEQ_CHUNK = 256
CARRY = 8


def _cparams(*sem):
    return pltpu.CompilerParams(dimension_semantics=sem, vmem_limit_bytes=VMEM_LIMIT)


def _sigmoid(x):
    return 1.0 / (1.0 + jnp.exp(-x))


def _silu(x):
    return x * _sigmoid(x)


def _softplus(x):
    return jnp.maximum(x, 0.0) + jnp.log(1.0 + jnp.exp(-jnp.abs(x)))


def _log_sigmoid(x):
    return jnp.minimum(x, 0.0) - jnp.log(1.0 + jnp.exp(-jnp.abs(x)))


def _split3(x):
    x1 = x.astype(BF16)
    r1 = x - x1.astype(F32)
    x2 = r1.astype(BF16)
    x3 = (r1 - x2.astype(F32)).astype(BF16)
    return x1, x2, x3


def _dot(a, b):
    return jnp.dot(a, b, preferred_element_type=F32)


def _dot_nt(a, b):
    return lax.dot_general(a, b, (((1,), (1,)), ((), ())), preferred_element_type=F32)


def _dot_tn(a, b):
    return lax.dot_general(a, b, (((0,), (0,)), ((), ())), preferred_element_type=F32)


def _dot_f32_left(x, m01):
    x1, x2, x3 = _split3(x)
    return _dot(x1, m01) + _dot(x2, m01) + _dot(x3, m01)


def _dot_f32_right(m01, x):
    x1, x2, x3 = _split3(x)
    return _dot(m01, x1) + _dot(m01, x2) + _dot(m01, x3)


def _rmsnorm_rows(x, g):
    ms = jnp.mean(x * x, axis=-1, keepdims=True)
    return (x * lax.rsqrt(ms + EPS)) * g


def _norm_small_kernel(x_ref, g_ref, w0_ref, w1_ref, h_ref, s_ref, st_ref):
    hb = _rmsnorm_rows(x_ref[...], g_ref[...]).astype(BF16)
    h_ref[...] = hb
    wt = jnp.concatenate([w0_ref[0], w1_ref[0]], axis=0).astype(BF16)
    s_ref[...] = _dot_nt(hb, wt)
    st_ref[...] = _dot_nt(wt, hb)


def _norm_small(x, g, w_in_t, layer, *, tm=1024):
    t, d = x.shape
    return pl.pallas_call(
        _norm_small_kernel,
        grid=(t // tm,),
        in_specs=[
            pl.BlockSpec((tm, d), lambda i: (i, 0)),
            pl.BlockSpec((1, d), lambda i: (0, 0)),
            pl.BlockSpec((1, SM_W, d), lambda i: (layer, SRC_IF // SM_W, 0)),
            pl.BlockSpec((1, SM_W, d), lambda i: (layer, SRC_DT // SM_W, 0)),
        ],
        out_specs=[
            pl.BlockSpec((tm, d), lambda i: (i, 0)),
            pl.BlockSpec((tm, SM_COLS), lambda i: (i, 0)),
            pl.BlockSpec((SM_COLS, tm), lambda i: (0, i)),
        ],
        out_shape=[
            jax.ShapeDtypeStruct((t, d), BF16),
            jax.ShapeDtypeStruct((t, SM_COLS), F32),
            jax.ShapeDtypeStruct((SM_COLS, t), F32),
        ],
        compiler_params=_cparams("parallel"),
        name="norm_small",
    )(x, g, w_in_t, w_in_t)


CAST_ROWS = 256


def _conv_silu(acc, carry, cw_ref, cb_ref):
    cat = jnp.concatenate([carry, acc], axis=0)
    y = cb_ref[...] + cw_ref[CONV_K - 1:CONV_K, :] * acc
    for back in range(1, CONV_K):
        tap = pltpu.roll(cat, back, axis=0)[CARRY:, :]
        y = y + cw_ref[CONV_K - 1 - back:CONV_K - back, :] * tap
    return _silu(y)


PROJ_SUB_ROWS = 512


def _proj_kernel(h_ref, wt_ref, cw_ref, cb_ref, o_ref, ws_ref, carry_ref, acc_ref, *, tiles_per_seq):
    j = pl.program_id(0)
    i = pl.program_id(1)
    tn = ws_ref.shape[1]

    @pl.when(i == 0)
    def _():
        for r in range(0, tn, CAST_ROWS):
            ws_ref[:, r:r + CAST_ROWS] = wt_ref[0, r:r + CAST_ROWS, :].T.astype(BF16)

    @pl.when(i % tiles_per_seq == 0)
    def _():
        carry_ref[...] = jnp.zeros_like(carry_ref)

    is_conv = jnp.logical_or(j < PC_V // tn, jnp.logical_and(j >= PC_XBC // tn, j < PC_GATE // tn))
    is_sigmoid = jnp.logical_and(j >= PC_O // tn, j < PC_Z // tn)
    is_silu = jnp.logical_and(j >= PC_Z // tn, j < PC_XBC // tn)
    is_plain = jnp.logical_not(jnp.logical_or(is_conv, jnp.logical_or(is_sigmoid, is_silu)))

    sub = PROJ_SUB_ROWS
    row_blocks = [slice(r, r + sub) for r in range(0, h_ref.shape[0], sub)]

    def sub_dot(s, rows):
        acc_ref[s % 2] = _dot(h_ref[rows, :], ws_ref[...])
        return acc_ref[s % 2]

    @pl.when(is_conv)
    def _():
        carry = carry_ref[...]
        for s, rows in enumerate(row_blocks):
            acc = sub_dot(s, rows)
            o_ref[rows, :] = _conv_silu(acc, carry, cw_ref, cb_ref).astype(o_ref.dtype)
            carry = acc[sub - CARRY:, :]
        carry_ref[...] = carry

    @pl.when(is_sigmoid)
    def _():
        for s, rows in enumerate(row_blocks):
            o_ref[rows, :] = _sigmoid(sub_dot(s, rows)).astype(o_ref.dtype)

    @pl.when(is_silu)
    def _():
        for s, rows in enumerate(row_blocks):
            o_ref[rows, :] = _silu(sub_dot(s, rows)).astype(o_ref.dtype)

    @pl.when(is_plain)
    def _():
        for rows in row_blocks:
            o_ref[rows, :] = _dot(h_ref[rows, :], ws_ref[...]).astype(o_ref.dtype)


def _proj_src_row(j, tn):
    past_z = (j >= PC_Z // tn).astype(jnp.int32)
    past_gate = (j >= PC_GATE // tn).astype(jnp.int32)
    shift8 = past_z * (SHIFT_ZXBC // 8) + past_gate * ((SHIFT_GATE - SHIFT_ZXBC) // 8)
    return pl.multiple_of(j * tn + 8 * shift8, 8)


def _proj(h, w_in_t, layer, conv_w, conv_b, *, seq, tm=2048):
    t, d = h.shape
    tn = PROJ_TN
    assert t % tm == 0 and seq % tm == 0, (t, seq, tm)
    return pl.pallas_call(
        functools.partial(_proj_kernel, tiles_per_seq=seq // tm),
        grid=(PC_TOTAL // tn, t // tm),
        in_specs=[
            pl.BlockSpec((tm, d), lambda j, i: (i, 0)),
            pl.BlockSpec((pl.Element(1), pl.Element(tn), pl.Element(d)),
                         lambda j, i: (layer, _proj_src_row(j, tn), 0)),
            pl.BlockSpec((CONV_K, tn), lambda j, i: (0, j)),
            pl.BlockSpec((1, tn), lambda j, i: (0, j)),
        ],
        out_specs=pl.BlockSpec((tm, tn), lambda j, i: (i, j)),
        out_shape=jax.ShapeDtypeStruct((t, PC_TOTAL), BF16),
        scratch_shapes=[pltpu.VMEM((d, tn), BF16), pltpu.VMEM((CARRY, tn), F32),
                        pltpu.VMEM((2, PROJ_SUB_ROWS, tn), F32)],
        compiler_params=_cparams("parallel", "arbitrary"),
        name="proj",
    )(h, w_in_t, conv_w, conv_b)


def _mlstm_kernel(q_ref, k_ref, v_ref, o_ref, sm_ref, smt_ref, ifr_ref, ifc_ref,
                  g_ref, tril_ref, triu_ref, y_ref, c_ref, n_ref, m_ref):
    length = q_ref.shape[0]

    @pl.when(pl.program_id(1) == 0)
    def _():
        c_ref[...] = jnp.zeros_like(c_ref)
        n_ref[...] = jnp.zeros_like(n_ref)
        m_ref[...] = jnp.zeros_like(m_ref)

    cols = sm_ref[...] + ifr_ref[...]
    rows = smt_ref[...] + ifc_ref[...]
    b_cols = _dot_f32_right(tril_ref[...], _log_sigmoid(cols))
    b_rows = _dot_f32_left(_log_sigmoid(rows), triu_ref[...])
    r_i = lax.broadcasted_iota(jnp.int32, (length, length), 0)
    c_i = lax.broadcasted_iota(jnp.int32, (length, length), 1)
    causal = r_i >= c_i

    for h in range(ML_HEADS):
        kb = k_ref[:, h * ML_DK:(h + 1) * ML_DK]
        q_h = q_ref[:, h * ML_DK:(h + 1) * ML_DK].astype(F32) * (ML_DK ** -0.5)
        k_h = kb.astype(F32)
        qb = q_h.astype(BF16)
        v_h = v_ref[:, h * ML_DV:(h + 1) * ML_DV]
        i_col = cols[:, h:h + 1]
        b_col = b_cols[:, ML_HEADS + h:ML_HEADS + h + 1]
        i_row = rows[h:h + 1, :]
        b_row = b_rows[ML_HEADS + h:ML_HEADS + h + 1, :]
        m_prev = m_ref[h:h + 1, 0:1]
        c_prev = c_ref[h]
        n_prev = n_ref[h:h + 1, :]

        log_d = jnp.where(causal, b_col - b_row + i_row, -jnp.inf)
        m_inter = b_col + m_prev
        m_t = jnp.maximum(m_inter, jnp.max(log_d, axis=-1, keepdims=True))
        scores = _dot_nt(qb, kb) * jnp.exp(log_d - m_t)
        inter = jnp.exp(m_inter - m_t)
        num = _dot(scores.astype(BF16), v_h) + inter * _dot(qb, c_prev.astype(BF16))
        den = (jnp.sum(scores, axis=-1, keepdims=True)
               + inter * jnp.sum(q_h * n_prev, axis=-1, keepdims=True))
        hh = num / jnp.maximum(jnp.abs(den), jnp.exp(-m_t))
        hh = hh * lax.rsqrt(jnp.mean(hh * hh, axis=-1, keepdims=True) + EPS)
        gate = o_ref[:, h * ML_DV:(h + 1) * ML_DV].astype(F32)
        y_ref[:, h * ML_DV:(h + 1) * ML_DV] = (
            hh * g_ref[:, h * ML_DV:(h + 1) * ML_DV] * gate).astype(y_ref.dtype)

        b_last = b_col[length - 1:length, :]
        log_w_row = b_last - b_row + i_row
        m_new = jnp.maximum(b_last + m_prev, jnp.max(log_w_row, axis=-1, keepdims=True))
        w_col = jnp.exp(b_last - b_col + i_col - m_new)
        decay = jnp.exp(b_last + m_prev - m_new)
        vw = (v_h.astype(F32) * w_col).astype(BF16)
        c_ref[h] = decay * c_prev + _dot_tn(kb, vw)
        n_ref[h:h + 1, :] = decay * n_prev + jnp.sum(k_h * w_col, axis=0, keepdims=True)
        m_ref[h:h + 1, :] = jnp.broadcast_to(m_new, (1, m_ref.shape[1]))


def _mlstm(proj, small, small_t, if_row, if_col, norm_g, tril, triu, *, bsz, seq):
    length = SEQ_CHUNK
    nc = seq // length
    t = bsz * seq
    row = lambda b, c: b * nc + c
    return pl.pallas_call(
        _mlstm_kernel,
        grid=(bsz, nc),
        in_specs=[
            pl.BlockSpec((length, ML_QK), lambda b, c: (row(b, c), PC_Q // ML_QK)),
            pl.BlockSpec((length, ML_QK), lambda b, c: (row(b, c), PC_K // ML_QK)),
            pl.BlockSpec((length, ML_WIDTH), lambda b, c: (row(b, c), PC_V // ML_WIDTH)),
            pl.BlockSpec((length, ML_WIDTH), lambda b, c: (row(b, c), PC_O // ML_WIDTH)),
            pl.BlockSpec((length, SM_W), lambda b, c: (row(b, c), 0)),
            pl.BlockSpec((2 * ML_HEADS, length), lambda b, c: (0, row(b, c))),
            pl.BlockSpec((1, SM_W), lambda b, c: (0, 0)),
            pl.BlockSpec((2 * ML_HEADS, length), lambda b, c: (0, 0)),
            pl.BlockSpec((1, ML_WIDTH), lambda b, c: (0, 0)),
            pl.BlockSpec((length, length), lambda b, c: (0, 0)),
            pl.BlockSpec((length, length), lambda b, c: (0, 0)),
        ],
        out_specs=pl.BlockSpec((length, ML_WIDTH), lambda b, c: (row(b, c), 0)),
        out_shape=jax.ShapeDtypeStruct((t, ML_WIDTH), BF16),
        scratch_shapes=[
            pltpu.VMEM((ML_HEADS, ML_DK, ML_DV), F32),
            pltpu.VMEM((2 * ML_HEADS, ML_DK), F32),
            pltpu.VMEM((2 * ML_HEADS, 128), F32),
        ],
        compiler_params=_cparams("parallel", "arbitrary"),
        name="mlstm",
    )(proj, proj, proj, proj, small, small_t, if_row, if_col, norm_g, tril, triu)


def _ssd_kernel(xbc_ref, z_ref, sm_ref, smt_ref, dtb_r_ref, dtb_c_ref, al_r_ref,
                al_c_ref, dsk_ref, g_ref, tril_ref, triu_ref, e_ref, y_ref, st_ref):
    length = xbc_ref.shape[0]

    @pl.when(pl.program_id(1) == 0)
    def _():
        st_ref[...] = jnp.zeros_like(st_ref)

    lane = lax.broadcasted_iota(jnp.int32, (1, SM_W), 1)
    head_lane = jnp.logical_and(lane >= SM_DT, lane < SM_DT + SSD_HEADS)
    dt_c = jnp.where(head_lane, _softplus(sm_ref[...] + dtb_r_ref[...]), 0.0)
    dt_r = _softplus(smt_ref[SM_DT:SM_DT + SSD_HEADS, :] + dtb_c_ref[...])
    a_c = dt_c * (-jnp.exp(al_r_ref[...]))
    a_r = dt_r * (-jnp.exp(al_c_ref[...]))
    acum_c = _dot_f32_right(tril_ref[...], a_c)
    acum_r = _dot_f32_left(a_r, triu_ref[...])
    a_last = acum_c[length - 1:length, :]

    e01 = e_ref[...]
    dt_e = _dot_f32_left(dt_c, e01)
    from_start_e = _dot_f32_left(jnp.exp(acum_c), e01)
    to_end_e = _dot_f32_left(jnp.exp(a_last - acum_c), e01)
    chunk_e = _dot_f32_left(jnp.broadcast_to(jnp.exp(a_last), (8, SM_W)), e01)[0:1, :]

    r_i = lax.broadcasted_iota(jnp.int32, (length, length), 0)
    c_i = lax.broadcasted_iota(jnp.int32, (length, length), 1)
    causal = r_i >= c_i

    for g in range(SSD_GROUPS):
        ch = slice(g * SSD_GW, (g + 1) * SSD_GW)
        xs_g = xbc_ref[:, ch].astype(F32)
        bm = xbc_ref[:, SSD_WIDTH + g * SSD_N:SSD_WIDTH + (g + 1) * SSD_N]
        cm = xbc_ref[:, SSD_WIDTH + SSD_GN + g * SSD_N:SSD_WIDTH + SSD_GN + (g + 1) * SSD_N]
        xdt = xs_g * dt_e[:, ch]
        cb = _dot_nt(cm, bm)
        prev = st_ref[g]
        y = _dot(cm, prev.astype(BF16)) * from_start_e[:, ch] + dsk_ref[:, ch] * xs_g
        diag = []
        for r in range(SSD_R):
            h = g * SSD_R + r
            seg = acum_c[:, SM_DT + h:SM_DT + h + 1] - acum_r[h:h + 1, :]
            att = cb * jnp.exp(jnp.where(causal, seg, -jnp.inf))
            diag.append(_dot(att.astype(BF16), xdt[:, r * SSD_P:(r + 1) * SSD_P].astype(BF16)))
        y = y + jnp.concatenate(diag, axis=-1)
        st_ref[g] = prev * chunk_e[:, ch] + _dot_tn(bm, (xdt * to_end_e[:, ch]).astype(BF16))
        y = y * z_ref[:, ch].astype(F32)
        y = y * lax.rsqrt(jnp.mean(y * y, axis=-1, keepdims=True) + EPS)
        y_ref[:, ch] = (y * g_ref[:, ch]).astype(y_ref.dtype)


def _ssd(proj, small, small_t, dtb_row, dtb_col, al_row, al_col, dskip_e, norm_g,
         tril, triu, expand, *, bsz, seq):
    length = SEQ_CHUNK
    nc = seq // length
    t = bsz * seq
    row = lambda b, c: b * nc + c
    const = lambda b, c: (0, 0)
    return pl.pallas_call(
        _ssd_kernel,
        grid=(bsz, nc),
        in_specs=[
            pl.BlockSpec((length, SSD_CONV), lambda b, c: (row(b, c), PC_XBC // SSD_CONV)),
            pl.BlockSpec((length, SSD_WIDTH), lambda b, c: (row(b, c), PC_Z // SSD_WIDTH)),
            pl.BlockSpec((length, SM_W), lambda b, c: (row(b, c), 1)),
            pl.BlockSpec((2 * SSD_HEADS, length), lambda b, c: (SM_W // (2 * SSD_HEADS), row(b, c))),
            pl.BlockSpec((1, SM_W), const),
            pl.BlockSpec((SSD_HEADS, length), const),
            pl.BlockSpec((1, SM_W), const),
            pl.BlockSpec((SSD_HEADS, length), const),
            pl.BlockSpec((1, SSD_WIDTH), const),
            pl.BlockSpec((1, SSD_WIDTH), const),
            pl.BlockSpec((length, length), const),
            pl.BlockSpec((length, length), const),
            pl.BlockSpec((SM_W, SSD_WIDTH), const),
        ],
        out_specs=pl.BlockSpec((length, SSD_WIDTH), lambda b, c: (row(b, c), 0)),
        out_shape=jax.ShapeDtypeStruct((t, SSD_WIDTH), BF16),
        scratch_shapes=[pltpu.VMEM((SSD_GROUPS, SSD_N, SSD_GW), F32)],
        compiler_params=_cparams("parallel", "arbitrary"),
        name="ssd",
    )(proj, proj, small, small_t, dtb_row, dtb_col, al_row, al_col, dskip_e, norm_g, tril, triu, expand)


def _cast_weight(w_ref, ws_ref):
    lead = (0,) * (len(w_ref.shape) - 2)
    for r in range(0, ws_ref.shape[0], CAST_ROWS):
        ws_ref[r:r + CAST_ROWS, :] = w_ref[lead + (slice(r, r + CAST_ROWS), slice(None))].astype(BF16)


def _merge_kernel(ya_ref, yb_ref, wa_ref, wb_ref, ga_ref, gb_ref, ba_ref, bb_ref, o_ref, was_ref, wbs_ref):
    @pl.when(pl.program_id(1) == 0)
    def _():
        _cast_weight(wa_ref, was_ref)
        _cast_weight(wb_ref, wbs_ref)

    ga = _sigmoid(ga_ref[...].astype(F32) + ba_ref[...])
    gb = _sigmoid(gb_ref[...].astype(F32) + bb_ref[...])
    o_ref[...] = (ga * _dot(ya_ref[...], was_ref[...])
                  + gb * _dot(yb_ref[...], wbs_ref[...])).astype(o_ref.dtype)


def _merge(ya, yb, wa, wb, layer, proj, gate_bias, *, tm=1024, tn=512):
    t, d = ya.shape
    n = wa.shape[2]
    ga0 = PC_GATE // tn
    gb0 = (PC_GATE + n) // tn
    return pl.pallas_call(
        _merge_kernel,
        grid=(n // tn, t // tm),
        in_specs=[
            pl.BlockSpec((tm, d), lambda j, i: (i, 0)),
            pl.BlockSpec((tm, d), lambda j, i: (i, 0)),
            pl.BlockSpec((1, d, tn), lambda j, i: (layer, 0, j)),
            pl.BlockSpec((1, d, tn), lambda j, i: (layer, 0, j)),
            pl.BlockSpec((tm, tn), lambda j, i: (i, ga0 + j)),
            pl.BlockSpec((tm, tn), lambda j, i: (i, gb0 + j)),
            pl.BlockSpec((1, tn), lambda j, i: (0, j)),
            pl.BlockSpec((1, tn), lambda j, i: (0, n // tn + j)),
        ],
        out_specs=pl.BlockSpec((tm, tn), lambda j, i: (i, j)),
        out_shape=jax.ShapeDtypeStruct((t, n), BF16),
        scratch_shapes=[pltpu.VMEM((d, tn), BF16), pltpu.VMEM((d, tn), BF16)],
        compiler_params=_cparams("parallel", "arbitrary"),
        name="merge",
    )(ya, yb, wa, wb, proj, proj, gate_bias, gate_bias)


def _matmul_resid_kernel(a_ref, w_ref, r_ref, o_ref, ws_ref):
    @pl.when(pl.program_id(1) == 0)
    def _():
        _cast_weight(w_ref, ws_ref)

    o_ref[...] = r_ref[...] + _dot(a_ref[...], ws_ref[...])


def _matmul_resid(a, w, layer, resid, *, tm, tn, name):
    t, k = a.shape
    n = w.shape[2]
    return pl.pallas_call(
        _matmul_resid_kernel,
        grid=(n // tn, t // tm),
        in_specs=[
            pl.BlockSpec((tm, k), lambda j, i: (i, 0)),
            pl.BlockSpec((1, k, tn), lambda j, i: (layer, 0, j)),
            pl.BlockSpec((tm, tn), lambda j, i: (i, j)),
        ],
        out_specs=pl.BlockSpec((tm, tn), lambda j, i: (i, j)),
        out_shape=jax.ShapeDtypeStruct((t, n), F32),
        scratch_shapes=[pltpu.VMEM((k, tn), BF16)],
        compiler_params=_cparams("parallel", "arbitrary"),
        name=name,
    )(a, w, resid)


def _ffn_gu_kernel(x_ref, g_ref, wg_ref, wu_ref, o_ref, h_ref):
    @pl.when(pl.program_id(1) == 0)
    def _():
        h_ref[...] = _rmsnorm_rows(x_ref[...], g_ref[...]).astype(BF16)

    h = h_ref[...]
    gate = _dot(h, wg_ref[0].astype(BF16))
    up = _dot(h, wu_ref[0].astype(BF16))
    o_ref[...] = (_silu(gate) * up).astype(o_ref.dtype)


def _ffn_gu(x, g, w_gu, layer, *, tm=1024, tf=512):
    t, d = x.shape
    f = w_gu.shape[2] // 2
    nf = f // tf
    return pl.pallas_call(
        _ffn_gu_kernel,
        grid=(t // tm, nf),
        in_specs=[
            pl.BlockSpec((tm, d), lambda i, j: (i, 0)),
            pl.BlockSpec((1, d), lambda i, j: (0, 0)),
            pl.BlockSpec((1, d, tf), lambda i, j: (layer, 0, j)),
            pl.BlockSpec((1, d, tf), lambda i, j: (layer, 0, nf + j)),
        ],
        out_specs=pl.BlockSpec((tm, tf), lambda i, j: (i, j)),
        out_shape=jax.ShapeDtypeStruct((t, f), BF16),
        scratch_shapes=[pltpu.VMEM((tm, d), BF16)],
        compiler_params=_cparams("parallel", "arbitrary"),
        name="ffn_gu",
    )(x, g, w_gu, w_gu)


def _pack_pairs(lo, hi):
    lo_bits = lax.bitcast_convert_type(lo.astype(BF16).astype(F32), jnp.uint32)
    hi_bits = lax.bitcast_convert_type(hi.astype(BF16).astype(F32), jnp.uint32)
    return (hi_bits & jnp.uint32(0xFFFF0000)) | (lo_bits >> 16)


def _unpack_pairs(packed):
    lo = lax.bitcast_convert_type(packed << 16, F32)
    hi = lax.bitcast_convert_type(packed & jnp.uint32(0xFFFF0000), F32)
    return lo, hi


def _router_kernel(x_ref, g_ref, wr_ref, br_ref, h_ref, r_ref):
    h = _rmsnorm_rows(x_ref[...], g_ref[...])
    half = h.shape[1] // 2
    h_ref[...] = _pack_pairs(h[:, :half], h[:, half:])
    h1, h2, h3 = _split3(h)
    w = wr_ref[...]
    w1, w2, w3 = _split3(w)
    logits = (_dot(h1, w1) + (_dot(h1, w2) + _dot(h2, w1))
              + (_dot(h1, w3) + _dot(h2, w2) + _dot(h3, w1))) + br_ref[...]
    lane = lax.broadcasted_iota(jnp.int32, logits.shape, 1)
    logits = jnp.where(lane < N_EXPERTS, logits, -jnp.inf)
    m1 = jnp.max(logits, axis=-1, keepdims=True)
    i1 = jnp.min(jnp.where(logits == m1, lane, SM_W), axis=-1, keepdims=True)
    rest = jnp.where(lane == i1, -jnp.inf, logits)
    m2 = jnp.max(rest, axis=-1, keepdims=True)
    i2 = jnp.min(jnp.where(rest == m2, lane, SM_W), axis=-1, keepdims=True)
    e = jnp.exp(m2 - m1)
    p1 = 1.0 / (1.0 + e)
    p2 = e / (1.0 + e)
    r_ref[...] = jnp.where(lane == 0, i1.astype(F32),
                           jnp.where(lane == 1, i2.astype(F32),
                                     jnp.where(lane == 2, p1, jnp.where(lane == 3, p2, 0.0))))


def _router(x, g, w_router, b_router, *, tm=512):
    t, d = x.shape
    return pl.pallas_call(
        _router_kernel,
        grid=(t // tm,),
        in_specs=[
            pl.BlockSpec((tm, d), lambda i: (i, 0)),
            pl.BlockSpec((1, d), lambda i: (0, 0)),
            pl.BlockSpec((d, SM_W), lambda i: (0, 0)),
            pl.BlockSpec((1, SM_W), lambda i: (0, 0)),
        ],
        out_specs=[
            pl.BlockSpec((tm, d // 2), lambda i: (i, 0)),
            pl.BlockSpec((tm, SM_W), lambda i: (i, 0)),
        ],
        out_shape=[
            jax.ShapeDtypeStruct((t, d // 2), jnp.uint32),
            jax.ShapeDtypeStruct((t, SM_W), F32),
        ],
        compiler_params=_cparams("parallel"),
        name="router",
    )(x, g, w_router, b_router)


def _row_copy(src_ref, dst_ref, sem, src_row, dst_row):
    return pltpu.make_async_copy(src_ref.at[pl.ds(src_row, 1), :], dst_ref.at[pl.ds(dst_row, 1), :], sem)


def _gather_kernel(idx_ref, src_ref, o_ref, buf_ref, sem):
    rows = buf_ref.shape[0]
    base = pl.program_id(0) * rows

    def start(pair, carry):
        for lane in range(2):
            r = 2 * pair + lane
            _row_copy(src_ref, buf_ref, sem, idx_ref[base + r], r).start(priority=lane)
        return carry

    lax.fori_loop(0, rows // 2, start, 0)

    def wait(r, carry):
        _row_copy(src_ref, buf_ref, sem, 0, r).wait()
        return carry

    lax.fori_loop(0, rows, wait, 0)
    lo, hi = _unpack_pairs(buf_ref[...])
    half = buf_ref.shape[1]
    o_ref[:, :half] = lo.astype(o_ref.dtype)
    o_ref[:, half:] = hi.astype(o_ref.dtype)


def _gather_rows(src, idx, *, rows=256):
    p = idx.shape[0]
    half = src.shape[1]
    d = 2 * half
    return pl.pallas_call(
        _gather_kernel,
        grid_spec=pltpu.PrefetchScalarGridSpec(
            num_scalar_prefetch=1,
            grid=(p // rows,),
            in_specs=[pl.BlockSpec(memory_space=pl.ANY)],
            out_specs=pl.BlockSpec((rows, d), lambda i, idx_ref: (i, 0)),
            scratch_shapes=[pltpu.VMEM((rows, half), src.dtype), pltpu.SemaphoreType.DMA(())],
        ),
        out_shape=jax.ShapeDtypeStruct((p, d), BF16),
        compiler_params=_cparams("arbitrary"),
        name="moe_gather",
    )(idx, src)


def _expert_changed(te_ref):
    i = pl.program_id(1)
    return jnp.logical_or(i == 0, te_ref[i] != te_ref[jnp.maximum(i - 1, 0)])


def _moe_gu_kernel(te_ref, nu_ref, a_ref, wg_ref, wu_ref, o_ref, wgs_ref, wus_ref):
    used = pl.program_id(1) < nu_ref[0]

    @pl.when(_expert_changed(te_ref))
    def _():
        _cast_weight(wg_ref, wgs_ref)
        _cast_weight(wu_ref, wus_ref)

    @pl.when(used)
    def _():
        a = a_ref[...]
        o_ref[...] = (_silu(_dot(a, wgs_ref[...])) * _dot(a, wus_ref[...])).astype(o_ref.dtype)

    @pl.when(jnp.logical_not(used))
    def _():
        o_ref[...] = jnp.zeros_like(o_ref)


def _moe_gu(a, w_gu, layer, tile_expert, n_used, *, tm, tf=1024):
    p, d = a.shape
    f = w_gu.shape[3] // 2
    nf = f // tf
    return pl.pallas_call(
        _moe_gu_kernel,
        grid_spec=pltpu.PrefetchScalarGridSpec(
            num_scalar_prefetch=2,
            grid=(nf, p // tm),
            in_specs=[
                pl.BlockSpec((tm, d), lambda j, i, te, nu: (i, 0)),
                pl.BlockSpec((1, 1, d, tf), lambda j, i, te, nu: (layer, te[i], 0, j)),
                pl.BlockSpec((1, 1, d, tf), lambda j, i, te, nu: (layer, te[i], 0, nf + j)),
            ],
            out_specs=pl.BlockSpec((tm, tf), lambda j, i, te, nu: (i, j)),
            scratch_shapes=[pltpu.VMEM((d, tf), BF16), pltpu.VMEM((d, tf), BF16)],
        ),
        out_shape=jax.ShapeDtypeStruct((p, f), BF16),
        compiler_params=_cparams("parallel", "arbitrary"),
        name="moe_gu",
    )(tile_expert, n_used, a, w_gu, w_gu)


def _moe_down_kernel(te_ref, nu_ref, a_ref, w_ref, o_ref, ws_ref):
    used = pl.program_id(1) < nu_ref[0]

    @pl.when(_expert_changed(te_ref))
    def _():
        _cast_weight(w_ref, ws_ref)

    @pl.when(used)
    def _():
        y = _dot(a_ref[...], ws_ref[...])
        half = y.shape[1] // 2
        o_ref[...] = _pack_pairs(y[:, :half], y[:, half:])

    @pl.when(jnp.logical_not(used))
    def _():
        o_ref[...] = jnp.zeros_like(o_ref)


def _moe_down(a, w_down, layer, tile_expert, n_used, *, tm, tn=MOE_DOWN_TN):
    p, f = a.shape
    n = w_down.shape[3]
    return pl.pallas_call(
        _moe_down_kernel,
        grid_spec=pltpu.PrefetchScalarGridSpec(
            num_scalar_prefetch=2,
            grid=(n // tn, p // tm),
            in_specs=[
                pl.BlockSpec((tm, f), lambda j, i, te, nu: (i, 0)),
                pl.BlockSpec((1, 1, f, tn), lambda j, i, te, nu: (layer, te[i], 0, j)),
            ],
            out_specs=pl.BlockSpec((tm, tn // 2), lambda j, i, te, nu: (i, j)),
            scratch_shapes=[pltpu.VMEM((f, tn), BF16)],
        ),
        out_shape=jax.ShapeDtypeStruct((p, n // 2), jnp.uint32),
        compiler_params=_cparams("parallel", "arbitrary"),
        name="moe_down",
    )(tile_expert, n_used, a, w_down)


def _combine_kernel(p1_ref, p2_ref, y_ref, x_ref, r_ref, o_ref, b1_ref, b2_ref, sem):
    rows = b1_ref.shape[0]
    base = pl.program_id(0) * rows

    def start(r, carry):
        _row_copy(y_ref, b1_ref, sem.at[0], p1_ref[base + r], r).start(priority=0)
        _row_copy(y_ref, b2_ref, sem.at[1], p2_ref[base + r], r).start(priority=1)
        return carry

    lax.fori_loop(0, rows, start, 0)

    def wait(r, carry):
        _row_copy(y_ref, b1_ref, sem.at[0], 0, r).wait()
        _row_copy(y_ref, b2_ref, sem.at[1], 0, r).wait()
        return carry

    lax.fori_loop(0, rows, wait, 0)
    route = r_ref[...]
    w1 = route[:, 2:3]
    w2 = route[:, 3:4]
    lo1, hi1 = _unpack_pairs(b1_ref[...])
    lo2, hi2 = _unpack_pairs(b2_ref[...])
    lo = w1 * lo1 + w2 * lo2
    hi = w1 * hi1 + w2 * hi2
    hw = MOE_DOWN_TN // 2
    for j in range(lo.shape[1] // hw):
        c = j * MOE_DOWN_TN
        o_ref[:, c:c + hw] = x_ref[:, c:c + hw] + lo[:, j * hw:(j + 1) * hw]
        o_ref[:, c + hw:c + 2 * hw] = x_ref[:, c + hw:c + 2 * hw] + hi[:, j * hw:(j + 1) * hw]


def _combine(y_sorted, x, route, pos1, pos2, *, rows=256):
    t, d = x.shape
    half = y_sorted.shape[1]
    return pl.pallas_call(
        _combine_kernel,
        grid_spec=pltpu.PrefetchScalarGridSpec(
            num_scalar_prefetch=2,
            grid=(t // rows,),
            in_specs=[
                pl.BlockSpec(memory_space=pl.ANY),
                pl.BlockSpec((rows, d), lambda i, p1, p2: (i, 0)),
                pl.BlockSpec((rows, SM_W), lambda i, p1, p2: (i, 0)),
            ],
            out_specs=pl.BlockSpec((rows, d), lambda i, p1, p2: (i, 0)),
            scratch_shapes=[pltpu.VMEM((rows, half), jnp.uint32), pltpu.VMEM((rows, half), jnp.uint32),
                            pltpu.SemaphoreType.DMA((2,))],
        ),
        out_shape=jax.ShapeDtypeStruct((t, d), F32),
        compiler_params=_cparams("arbitrary"),
        name="moe_combine",
    )(pos1, pos2, y_sorted, x, route)


MOE_TM = 512


def _moe_plan(route, tm):
    t = route.shape[0]
    n_tiles = (t * TOP_K) // tm + N_EXPERTS
    eid = route[:, :TOP_K].astype(jnp.int32).reshape(-1)
    onehot = (eid[:, None] == jnp.arange(N_EXPERTS, dtype=jnp.int32)[None, :]).astype(jnp.int32)
    csum = jnp.cumsum(onehot, axis=0)
    rank = jnp.sum((csum - onehot) * onehot, axis=1)
    counts = csum[-1]
    tiles_per = (counts + tm - 1) // tm
    tile_end = jnp.cumsum(tiles_per)
    start = (tile_end - tiles_per) * tm
    pos = (jnp.sum(onehot * start[None, :], axis=1) + rank).astype(jnp.int32)
    n_used = tile_end[-1].astype(jnp.int32)
    tile_id = jnp.arange(n_tiles, dtype=jnp.int32)
    tile_expert = jnp.sum((tile_id[:, None] >= tile_end[None, :]).astype(jnp.int32), axis=1)
    tile_expert = jnp.minimum(tile_expert, N_EXPERTS - 1).astype(jnp.int32)
    last_expert = jnp.max(jnp.where(counts > 0, jnp.arange(N_EXPERTS, dtype=jnp.int32), 0))
    tile_expert = jnp.where(tile_id < n_used, tile_expert, last_expert).astype(jnp.int32)
    token = jnp.repeat(jnp.arange(t, dtype=jnp.int32), TOP_K)
    row_token = jnp.zeros((n_tiles * tm,), jnp.int32).at[pos].set(token)
    pos = pos.reshape(t, TOP_K)
    return row_token, tile_expert, n_used.reshape(1), pos[:, 0], pos[:, 1]


def _moe_ffn(x, g, w_router, b_router, w_gu, w_down, layer):
    h, route = _router(x, g, w_router, b_router)
    row_token, tile_expert, n_used, pos1, pos2 = _moe_plan(route, MOE_TM)
    a = _gather_rows(h, row_token)
    act = _moe_gu(a, w_gu, layer, tile_expert, n_used, tm=MOE_TM)
    y = _moe_down(act, w_down, layer, tile_expert, n_used, tm=MOE_TM)
    return _combine(y, x, route, pos1, pos2)


def _final_norm_kernel(x_ref, g_ref, o_ref):
    o_ref[...] = _rmsnorm_rows(x_ref[...], g_ref[...])


def _final_norm(x, g, *, tm=512):
    t, d = x.shape
    return pl.pallas_call(
        _final_norm_kernel,
        grid=(t // tm,),
        in_specs=[pl.BlockSpec((tm, d), lambda i: (i, 0)), pl.BlockSpec((1, d), lambda i: (0, 0))],
        out_specs=pl.BlockSpec((tm, d), lambda i: (i, 0)),
        out_shape=jax.ShapeDtypeStruct((t, d), F32),
        compiler_params=_cparams("parallel"),
        name="final_norm",
    )(x, g)


def _lane_row(vec, offset):
    return jnp.zeros((1, SM_W), F32).at[0, offset:offset + vec.shape[0]].set(vec.astype(F32))


def _mixer(x, l, bsz, seq, p, consts):
    tril, triu, expand = consts
    h, small, small_t = _norm_small(x, p["norm_mix_g"][l][None, :], p["w_in_t"], l)
    conv_w = jnp.zeros((CONV_K, PC_TOTAL), F32)
    conv_w = conv_w.at[:, PC_Q:PC_V].set(p["qk_conv_w"][l]).at[:, PC_XBC:PC_GATE].set(p["ssd_conv_w"][l])
    conv_b = jnp.zeros((1, PC_TOTAL), F32)
    conv_b = conv_b.at[0, PC_Q:PC_V].set(p["qk_conv_b"][l]).at[0, PC_XBC:PC_GATE].set(p["ssd_conv_b"][l])
    proj = _proj(h, p["w_in_t"], l, conv_w, conv_b, seq=seq)

    if_bias = p["if_bias"][l]
    y_a = _mlstm(proj, small, small_t,
                 _lane_row(if_bias, 0),
                 jnp.broadcast_to(if_bias.astype(F32)[:, None], (2 * ML_HEADS, SEQ_CHUNK)),
                 p["mh_norm_g"][l][None, :], tril, triu, bsz=bsz, seq=seq)
    y_b = _ssd(proj, small, small_t,
               _lane_row(p["dt_bias"][l], SM_DT),
               jnp.broadcast_to(p["dt_bias"][l].astype(F32)[:, None], (SSD_HEADS, SEQ_CHUNK)),
               _lane_row(p["a_log"][l], SM_DT),
               jnp.broadcast_to(p["a_log"][l].astype(F32)[:, None], (SSD_HEADS, SEQ_CHUNK)),
               jnp.repeat(p["d_skip"][l].astype(F32), SSD_P)[None, :],
               p["ssd_norm_g"][l][None, :], tril, triu, expand, bsz=bsz, seq=seq)
    merged = _merge(y_a, y_b, p["w_branch_a"], p["w_branch_b"], l, proj, p["gate_bias"][l][None, :])
    return _matmul_resid(merged, p["w_out"], l, x, tm=1024, tn=1024, name="out_proj")


def kernel(x, norm_mix_g, w_in, if_bias, qk_conv_w, qk_conv_b, mh_norm_g, ssd_conv_w, ssd_conv_b,
           dt_bias, a_log, d_skip, ssd_norm_g, gate_bias, w_branch_a, w_branch_b, w_out, norm_ffn_g,
           ffn_w_gu, ffn_w_down, router_w, router_b, exp_w_gu, exp_w_down, norm_final_g):
    p = dict(norm_mix_g=norm_mix_g, w_in_t=jnp.swapaxes(w_in, 1, 2), if_bias=if_bias, qk_conv_w=qk_conv_w,
             qk_conv_b=qk_conv_b, mh_norm_g=mh_norm_g, ssd_conv_w=ssd_conv_w, ssd_conv_b=ssd_conv_b,
             dt_bias=dt_bias, a_log=a_log, d_skip=d_skip, ssd_norm_g=ssd_norm_g, gate_bias=gate_bias,
             w_branch_a=w_branch_a, w_branch_b=w_branch_b, w_out=w_out)
    bsz, seq, d = x.shape
    assert d == D_MODEL and seq % SEQ_CHUNK == 0 and (bsz * seq) % 2048 == 0, x.shape
    x = x.reshape(bsz * seq, d)

    idx = jnp.arange(SEQ_CHUNK, dtype=jnp.int32)
    tril = (idx[:, None] >= idx[None, :]).astype(BF16)
    triu = (idx[:, None] <= idx[None, :]).astype(BF16)
    lane = jnp.arange(SM_W, dtype=jnp.int32)[:, None]
    chan = jnp.arange(SSD_WIDTH, dtype=jnp.int32)[None, :]
    expand = (lane == SM_DT + chan // SSD_P).astype(BF16)
    consts = (tril, triu, expand)

    for l in range(DEPTH):
        x = _mixer(x, l, bsz, seq, p, consts)
        g = norm_ffn_g[l][None, :]
        if l % 2 == 0:
            act = _ffn_gu(x, g, ffn_w_gu, l // 2)
            x = _matmul_resid(act, ffn_w_down, l // 2, x, tm=512, tn=512, name="ffn_down")
        else:
            w_r = jnp.pad(router_w[l // 2], ((0, 0), (0, SM_W - N_EXPERTS)))
            b_r = _lane_row(router_b[l // 2], 0)
            x = _moe_ffn(x, g, w_r, b_r, exp_w_gu, exp_w_down, l // 2)
    return _final_norm(x, norm_final_g[None, :]).reshape(bsz, seq, d)
```

```python
import functools
import math

import jax
import jax.numpy as jnp
from jax import lax
from jax.experimental import pallas as pl
from jax.experimental.pallas import tpu as pltpu

F32 = jnp.float32
BF16 = jnp.bfloat16

D_MODEL = 2048
DEPTH = 4
EPS = 1e-6
CONV_K = 4
ML_HEADS = 4
ML_DV = 512
ML_DK = 256
ML_QK = ML_HEADS * ML_DK
ML_WIDTH = ML_HEADS * ML_DV
SSD_HEADS = 32
SSD_P = 64
SSD_GROUPS = 8
SSD_N = 128
SSD_R = SSD_HEADS // SSD_GROUPS
SSD_WIDTH = SSD_HEADS * SSD_P
SSD_GN = SSD_GROUPS * SSD_N
SSD_CONV = SSD_WIDTH + 2 * SSD_GN
SSD_GW = SSD_R * SSD_P
FFN_DENSE = 5632
N_EXPERTS = 8
TOP_K = 2
FFN_EXPERT = 4096

PC_Q = 0
PC_K = 1024
PC_V = 2048
PC_O = 4096
PC_Z = 6144
PC_XBC = 8192
PC_GATE = 12288
PC_TOTAL = 16384
SRC_IF = 6144
SRC_DT = 12296
SHIFT_ZXBC = 8
SHIFT_GATE = 40
PROJ_TN = 1024
SM_W = 128
SM_COLS = 2 * SM_W
SM_DT = SRC_DT - (SRC_DT // SM_W) * SM_W

V7X_VMEM_BYTES = 64 * 1024 * 1024
VMEM_LIMIT = 58 * 1024 * 1024

DMA_UNROLL = 16
MOE_DOWN_TN = 1024
SEQ_CHUNK = 256
CARRY = 8


def _cparams(*sem):
    return pltpu.CompilerParams(dimension_semantics=sem, vmem_limit_bytes=VMEM_LIMIT)


def _sigmoid(x):
    return 1.0 / (1.0 + jnp.exp(-x))


def _silu(x):
    return x * _sigmoid(x)


def _softplus(x):
    return jnp.maximum(x, 0.0) + jnp.log(1.0 + jnp.exp(-jnp.abs(x)))


def _log_sigmoid(x):
    return jnp.minimum(x, 0.0) - jnp.log(1.0 + jnp.exp(-jnp.abs(x)))


def _split3(x):
    x1 = x.astype(BF16)
    r1 = x - x1.astype(F32)
    x2 = r1.astype(BF16)
    x3 = (r1 - x2.astype(F32)).astype(BF16)
    return x1, x2, x3


def _dot(a, b):
    return jnp.dot(a, b, preferred_element_type=F32)


def _dot_nt(a, b):
    return lax.dot_general(a, b, (((1,), (1,)), ((), ())), preferred_element_type=F32)


def _dot_tn(a, b):
    return lax.dot_general(a, b, (((0,), (0,)), ((), ())), preferred_element_type=F32)


def _dot_f32_left(x, m01):
    x1, x2, x3 = _split3(x)
    return _dot(x1, m01) + _dot(x2, m01) + _dot(x3, m01)


def _dot_f32_right(m01, x):
    x1, x2, x3 = _split3(x)
    return _dot(m01, x1) + _dot(m01, x2) + _dot(m01, x3)


def _rmsnorm_rows(x, g):
    ms = jnp.mean(x * x, axis=-1, keepdims=True)
    return (x * lax.rsqrt(ms + EPS)) * g


def _norm_small_kernel(x_ref, g_ref, w0_ref, w1_ref, h_ref, s_ref, st_ref):
    hb = _rmsnorm_rows(x_ref[...], g_ref[...]).astype(BF16)
    h_ref[...] = hb
    wt = jnp.concatenate([w0_ref[0], w1_ref[0]], axis=0).astype(BF16)
    s_ref[...] = _dot_nt(hb, wt)
    st_ref[...] = _dot_nt(wt, hb)


def _norm_small(x, g, w_in_t, layer, *, tm=1024):
    t, d = x.shape
    return pl.pallas_call(
        _norm_small_kernel,
        grid=(t // tm,),
        in_specs=[
            pl.BlockSpec((tm, d), lambda i: (i, 0)),
            pl.BlockSpec((1, d), lambda i: (0, 0)),
            pl.BlockSpec((1, SM_W, d), lambda i: (layer, SRC_IF // SM_W, 0)),
            pl.BlockSpec((1, SM_W, d), lambda i: (layer, SRC_DT // SM_W, 0)),
        ],
        out_specs=[
            pl.BlockSpec((tm, d), lambda i: (i, 0)),
            pl.BlockSpec((tm, SM_COLS), lambda i: (i, 0)),
            pl.BlockSpec((SM_COLS, tm), lambda i: (0, i)),
        ],
        out_shape=[
            jax.ShapeDtypeStruct((t, d), BF16),
            jax.ShapeDtypeStruct((t, SM_COLS), F32),
            jax.ShapeDtypeStruct((SM_COLS, t), F32),
        ],
        compiler_params=_cparams("parallel"),
        name="norm_small",
    )(x, g, w_in_t, w_in_t)


CAST_ROWS = 256


def _conv_silu(acc, carry, cw_ref, cb_ref):
    cat = jnp.concatenate([carry, acc], axis=0)
    y = cb_ref[...] + cw_ref[CONV_K - 1:CONV_K, :] * acc
    for back in range(1, CONV_K):
        tap = pltpu.roll(cat, back, axis=0)[CARRY:, :]
        y = y + cw_ref[CONV_K - 1 - back:CONV_K - back, :] * tap
    return _silu(y)


PROJ_SUB_ROWS = 512


def _proj_kernel(h_ref, wt_ref, cw_ref, cb_ref, o_ref, ws_ref, carry_ref, acc_ref, *, tiles_per_seq):
    j = pl.program_id(0)
    i = pl.program_id(1)
    tn = ws_ref.shape[1]

    @pl.when(i == 0)
    def _():
        for r in range(0, tn, CAST_ROWS):
            ws_ref[:, r:r + CAST_ROWS] = wt_ref[0, r:r + CAST_ROWS, :].T.astype(BF16)

    @pl.when(i % tiles_per_seq == 0)
    def _():
        carry_ref[...] = jnp.zeros_like(carry_ref)

    is_conv = jnp.logical_or(j < PC_V // tn, jnp.logical_and(j >= PC_XBC // tn, j < PC_GATE // tn))
    is_sigmoid = jnp.logical_and(j >= PC_O // tn, j < PC_Z // tn)
    is_silu = jnp.logical_and(j >= PC_Z // tn, j < PC_XBC // tn)
    is_plain = jnp.logical_not(jnp.logical_or(is_conv, jnp.logical_or(is_sigmoid, is_silu)))

    sub = PROJ_SUB_ROWS
    row_blocks = [slice(r, r + sub) for r in range(0, h_ref.shape[0], sub)]

    def sub_dot(s):
        acc_ref[s % 2] = _dot(h_ref[row_blocks[s], :], ws_ref[...])

    def pipelined(epilogue):
        sub_dot(0)
        for s, rows in enumerate(row_blocks):
            if s + 1 < len(row_blocks):
                sub_dot(s + 1)
            o_ref[rows, :] = epilogue(acc_ref[s % 2]).astype(o_ref.dtype)

    @pl.when(is_conv)
    def _():
        carry = [carry_ref[...]]

        def conv(acc):
            y = _conv_silu(acc, carry[0], cw_ref, cb_ref)
            carry[0] = acc[sub - CARRY:, :]
            return y

        pipelined(conv)
        carry_ref[...] = carry[0]

    @pl.when(is_sigmoid)
    def _():
        pipelined(_sigmoid)

    @pl.when(is_silu)
    def _():
        pipelined(_silu)

    @pl.when(is_plain)
    def _():
        for rows in row_blocks:
            o_ref[rows, :] = _dot(h_ref[rows, :], ws_ref[...]).astype(o_ref.dtype)


def _proj_src_row(j, tn):
    past_z = (j >= PC_Z // tn).astype(jnp.int32)
    past_gate = (j >= PC_GATE // tn).astype(jnp.int32)
    shift8 = past_z * (SHIFT_ZXBC // 8) + past_gate * ((SHIFT_GATE - SHIFT_ZXBC) // 8)
    return pl.multiple_of(j * tn + 8 * shift8, 8)


def _proj(h, w_in_t, layer, conv_w, conv_b, *, seq, tm=2048):
    t, d = h.shape
    tn = PROJ_TN
    assert t % tm == 0 and seq % tm == 0, (t, seq, tm)
    return pl.pallas_call(
        functools.partial(_proj_kernel, tiles_per_seq=seq // tm),
        grid=(PC_TOTAL // tn, t // tm),
        in_specs=[
            pl.BlockSpec((tm, d), lambda j, i: (i, 0)),
            pl.BlockSpec((pl.Element(1), pl.Element(tn), pl.Element(d)),
                         lambda j, i: (layer, _proj_src_row(j, tn), 0)),
            pl.BlockSpec((CONV_K, tn), lambda j, i: (0, j)),
            pl.BlockSpec((1, tn), lambda j, i: (0, j)),
        ],
        out_specs=pl.BlockSpec((tm, tn), lambda j, i: (i, j)),
        out_shape=jax.ShapeDtypeStruct((t, PC_TOTAL), BF16),
        scratch_shapes=[pltpu.VMEM((d, tn), BF16), pltpu.VMEM((CARRY, tn), F32),
                        pltpu.VMEM((2, PROJ_SUB_ROWS, tn), F32)],
        compiler_params=_cparams("parallel", "arbitrary"),
        name="proj",
    )(h, w_in_t, conv_w, conv_b)


def _mlstm_kernel(q_ref, k_ref, v_ref, o_ref, sm_ref, smt_ref, ifr_ref, ifc_ref,
                  g_ref, tril_ref, triu_ref, y_ref, c_ref, n_ref, m_ref):
    length = q_ref.shape[0]

    @pl.when(pl.program_id(1) == 0)
    def _():
        c_ref[...] = jnp.zeros_like(c_ref)
        n_ref[...] = jnp.zeros_like(n_ref)
        m_ref[...] = jnp.zeros_like(m_ref)

    cols = sm_ref[...] + ifr_ref[...]
    rows = smt_ref[...] + ifc_ref[...]
    b_cols = _dot_f32_right(tril_ref[...], _log_sigmoid(cols))
    b_rows = _dot_f32_left(_log_sigmoid(rows), triu_ref[...])
    r_i = lax.broadcasted_iota(jnp.int32, (length, length), 0)
    c_i = lax.broadcasted_iota(jnp.int32, (length, length), 1)
    causal = r_i >= c_i

    for h in range(ML_HEADS):
        kb = k_ref[:, h * ML_DK:(h + 1) * ML_DK]
        q_h = q_ref[:, h * ML_DK:(h + 1) * ML_DK].astype(F32) * (ML_DK ** -0.5)
        k_h = kb.astype(F32)
        qb = q_h.astype(BF16)
        v_h = v_ref[:, h * ML_DV:(h + 1) * ML_DV]
        i_col = cols[:, h:h + 1]
        b_col = b_cols[:, ML_HEADS + h:ML_HEADS + h + 1]
        i_row = rows[h:h + 1, :]
        b_row = b_rows[ML_HEADS + h:ML_HEADS + h + 1, :]
        m_prev = m_ref[h:h + 1, 0:1]
        c_prev = c_ref[h]
        n_prev = n_ref[h:h + 1, :]

        log_d = jnp.where(causal, b_col - b_row + i_row, -jnp.inf)
        m_inter = b_col + m_prev
        m_t = jnp.maximum(m_inter, jnp.max(log_d, axis=-1, keepdims=True))
        scores = _dot_nt(qb, kb) * jnp.exp(log_d - m_t)
        inter = jnp.exp(m_inter - m_t)
        num = _dot(scores.astype(BF16), v_h) + inter * _dot(qb, c_prev.astype(BF16))
        den = (jnp.sum(scores, axis=-1, keepdims=True)
               + inter * jnp.sum(q_h * n_prev, axis=-1, keepdims=True))
        hh = num / jnp.maximum(jnp.abs(den), jnp.exp(-m_t))
        hh = hh * lax.rsqrt(jnp.mean(hh * hh, axis=-1, keepdims=True) + EPS)
        gate = o_ref[:, h * ML_DV:(h + 1) * ML_DV].astype(F32)
        y_ref[:, h * ML_DV:(h + 1) * ML_DV] = (
            hh * g_ref[:, h * ML_DV:(h + 1) * ML_DV] * gate).astype(y_ref.dtype)

        b_last = b_col[length - 1:length, :]
        log_w_row = b_last - b_row + i_row
        m_new = jnp.maximum(b_last + m_prev, jnp.max(log_w_row, axis=-1, keepdims=True))
        w_col = jnp.exp(b_last - b_col + i_col - m_new)
        decay = jnp.exp(b_last + m_prev - m_new)
        vw = (v_h.astype(F32) * w_col).astype(BF16)
        c_ref[h] = decay * c_prev + _dot_tn(kb, vw)
        n_ref[h:h + 1, :] = decay * n_prev + jnp.sum(k_h * w_col, axis=0, keepdims=True)
        m_ref[h:h + 1, :] = jnp.broadcast_to(m_new, (1, m_ref.shape[1]))


def _mlstm(proj, small, small_t, if_row, if_col, norm_g, tril, triu, *, bsz, seq):
    length = SEQ_CHUNK
    nc = seq // length
    t = bsz * seq
    row = lambda b, c: b * nc + c
    return pl.pallas_call(
        _mlstm_kernel,
        grid=(bsz, nc),
        in_specs=[
            pl.BlockSpec((length, ML_QK), lambda b, c: (row(b, c), PC_Q // ML_QK)),
            pl.BlockSpec((length, ML_QK), lambda b, c: (row(b, c), PC_K // ML_QK)),
            pl.BlockSpec((length, ML_WIDTH), lambda b, c: (row(b, c), PC_V // ML_WIDTH)),
            pl.BlockSpec((length, ML_WIDTH), lambda b, c: (row(b, c), PC_O // ML_WIDTH)),
            pl.BlockSpec((length, SM_W), lambda b, c: (row(b, c), 0)),
            pl.BlockSpec((2 * ML_HEADS, length), lambda b, c: (0, row(b, c))),
            pl.BlockSpec((1, SM_W), lambda b, c: (0, 0)),
            pl.BlockSpec((2 * ML_HEADS, length), lambda b, c: (0, 0)),
            pl.BlockSpec((1, ML_WIDTH), lambda b, c: (0, 0)),
            pl.BlockSpec((length, length), lambda b, c: (0, 0)),
            pl.BlockSpec((length, length), lambda b, c: (0, 0)),
        ],
        out_specs=pl.BlockSpec((length, ML_WIDTH), lambda b, c: (row(b, c), 0)),
        out_shape=jax.ShapeDtypeStruct((t, ML_WIDTH), BF16),
        scratch_shapes=[
            pltpu.VMEM((ML_HEADS, ML_DK, ML_DV), F32),
            pltpu.VMEM((2 * ML_HEADS, ML_DK), F32),
            pltpu.VMEM((2 * ML_HEADS, 128), F32),
        ],
        compiler_params=_cparams("parallel", "arbitrary"),
        name="mlstm",
    )(proj, proj, proj, proj, small, small_t, if_row, if_col, norm_g, tril, triu)


def _ssd_kernel(xbc_ref, z_ref, sm_ref, smt_ref, dtb_r_ref, dtb_c_ref, al_r_ref,
                al_c_ref, dsk_ref, g_ref, tril_ref, triu_ref, e_ref, y_ref, st_ref):
    length = xbc_ref.shape[0]

    @pl.when(pl.program_id(1) == 0)
    def _():
        st_ref[...] = jnp.zeros_like(st_ref)

    lane = lax.broadcasted_iota(jnp.int32, (1, SM_W), 1)
    head_lane = jnp.logical_and(lane >= SM_DT, lane < SM_DT + SSD_HEADS)
    dt_c = jnp.where(head_lane, _softplus(sm_ref[...] + dtb_r_ref[...]), 0.0)
    dt_r = _softplus(smt_ref[SM_DT:SM_DT + SSD_HEADS, :] + dtb_c_ref[...])
    a_c = dt_c * (-jnp.exp(al_r_ref[...]))
    a_r = dt_r * (-jnp.exp(al_c_ref[...]))
    acum_c = _dot_f32_right(tril_ref[...], a_c)
    acum_r = _dot_f32_left(a_r, triu_ref[...])
    a_last = acum_c[length - 1:length, :]

    e01 = e_ref[...]
    dt_e = _dot_f32_left(dt_c, e01)
    from_start_e = _dot_f32_left(jnp.exp(acum_c), e01)
    to_end_e = _dot_f32_left(jnp.exp(a_last - acum_c), e01)
    chunk_e = _dot_f32_left(jnp.broadcast_to(jnp.exp(a_last), (8, SM_W)), e01)[0:1, :]

    r_i = lax.broadcasted_iota(jnp.int32, (length, length), 0)
    c_i = lax.broadcasted_iota(jnp.int32, (length, length), 1)
    causal = r_i >= c_i

    for g in range(SSD_GROUPS):
        ch = slice(g * SSD_GW, (g + 1) * SSD_GW)
        xs_g = xbc_ref[:, ch].astype(F32)
        bm = xbc_ref[:, SSD_WIDTH + g * SSD_N:SSD_WIDTH + (g + 1) * SSD_N]
        cm = xbc_ref[:, SSD_WIDTH + SSD_GN + g * SSD_N:SSD_WIDTH + SSD_GN + (g + 1) * SSD_N]
        xdt = xs_g * dt_e[:, ch]
        cb = _dot_nt(cm, bm)
        prev = st_ref[g]
        y = _dot(cm, prev.astype(BF16)) * from_start_e[:, ch] + dsk_ref[:, ch] * xs_g
        diag = []
        for r in range(SSD_R):
            h = g * SSD_R + r
            seg = acum_c[:, SM_DT + h:SM_DT + h + 1] - acum_r[h:h + 1, :]
            att = cb * jnp.exp(jnp.where(causal, seg, -jnp.inf))
            diag.append(_dot(att.astype(BF16), xdt[:, r * SSD_P:(r + 1) * SSD_P].astype(BF16)))
        y = y + jnp.concatenate(diag, axis=-1)
        st_ref[g] = prev * chunk_e[:, ch] + _dot_tn(bm, (xdt * to_end_e[:, ch]).astype(BF16))
        y = y * z_ref[:, ch].astype(F32)
        y = y * lax.rsqrt(jnp.mean(y * y, axis=-1, keepdims=True) + EPS)
        y_ref[:, ch] = (y * g_ref[:, ch]).astype(y_ref.dtype)


def _ssd(proj, small, small_t, dtb_row, dtb_col, al_row, al_col, dskip_e, norm_g,
         tril, triu, expand, *, bsz, seq):
    length = SEQ_CHUNK
    nc = seq // length
    t = bsz * seq
    row = lambda b, c: b * nc + c
    const = lambda b, c: (0, 0)
    return pl.pallas_call(
        _ssd_kernel,
        grid=(bsz, nc),
        in_specs=[
            pl.BlockSpec((length, SSD_CONV), lambda b, c: (row(b, c), PC_XBC // SSD_CONV)),
            pl.BlockSpec((length, SSD_WIDTH), lambda b, c: (row(b, c), PC_Z // SSD_WIDTH)),
            pl.BlockSpec((length, SM_W), lambda b, c: (row(b, c), 1)),
            pl.BlockSpec((2 * SSD_HEADS, length), lambda b, c: (SM_W // (2 * SSD_HEADS), row(b, c))),
            pl.BlockSpec((1, SM_W), const),
            pl.BlockSpec((SSD_HEADS, length), const),
            pl.BlockSpec((1, SM_W), const),
            pl.BlockSpec((SSD_HEADS, length), const),
            pl.BlockSpec((1, SSD_WIDTH), const),
            pl.BlockSpec((1, SSD_WIDTH), const),
            pl.BlockSpec((length, length), const),
            pl.BlockSpec((length, length), const),
            pl.BlockSpec((SM_W, SSD_WIDTH), const),
        ],
        out_specs=pl.BlockSpec((length, SSD_WIDTH), lambda b, c: (row(b, c), 0)),
        out_shape=jax.ShapeDtypeStruct((t, SSD_WIDTH), BF16),
        scratch_shapes=[pltpu.VMEM((SSD_GROUPS, SSD_N, SSD_GW), F32)],
        compiler_params=_cparams("parallel", "arbitrary"),
        name="ssd",
    )(proj, proj, small, small_t, dtb_row, dtb_col, al_row, al_col, dskip_e, norm_g, tril, triu, expand)


def _cast_weight(w_ref, ws_ref):
    lead = (0,) * (len(w_ref.shape) - 2)
    for r in range(0, ws_ref.shape[0], CAST_ROWS):
        ws_ref[r:r + CAST_ROWS, :] = w_ref[lead + (slice(r, r + CAST_ROWS), slice(None))].astype(BF16)


def _merge_kernel(ya_ref, yb_ref, wa_ref, wb_ref, ga_ref, gb_ref, ba_ref, bb_ref, o_ref, was_ref, wbs_ref):
    @pl.when(pl.program_id(1) == 0)
    def _():
        _cast_weight(wa_ref, was_ref)
        _cast_weight(wb_ref, wbs_ref)

    ga = _sigmoid(ga_ref[...].astype(F32) + ba_ref[...])
    gb = _sigmoid(gb_ref[...].astype(F32) + bb_ref[...])
    o_ref[...] = (ga * _dot(ya_ref[...], was_ref[...])
                  + gb * _dot(yb_ref[...], wbs_ref[...])).astype(o_ref.dtype)


def _merge(ya, yb, wa, wb, layer, proj, gate_bias, *, tm=1024, tn=512):
    t, d = ya.shape
    n = wa.shape[2]
    ga0 = PC_GATE // tn
    gb0 = (PC_GATE + n) // tn
    return pl.pallas_call(
        _merge_kernel,
        grid=(n // tn, t // tm),
        in_specs=[
            pl.BlockSpec((tm, d), lambda j, i: (i, 0)),
            pl.BlockSpec((tm, d), lambda j, i: (i, 0)),
            pl.BlockSpec((1, d, tn), lambda j, i: (layer, 0, j)),
            pl.BlockSpec((1, d, tn), lambda j, i: (layer, 0, j)),
            pl.BlockSpec((tm, tn), lambda j, i: (i, ga0 + j)),
            pl.BlockSpec((tm, tn), lambda j, i: (i, gb0 + j)),
            pl.BlockSpec((1, tn), lambda j, i: (0, j)),
            pl.BlockSpec((1, tn), lambda j, i: (0, n // tn + j)),
        ],
        out_specs=pl.BlockSpec((tm, tn), lambda j, i: (i, j)),
        out_shape=jax.ShapeDtypeStruct((t, n), BF16),
        scratch_shapes=[pltpu.VMEM((d, tn), BF16), pltpu.VMEM((d, tn), BF16)],
        compiler_params=_cparams("parallel", "arbitrary"),
        name="merge",
    )(ya, yb, wa, wb, proj, proj, gate_bias, gate_bias)


def _matmul_resid_kernel(a_ref, w_ref, r_ref, o_ref, ws_ref):
    @pl.when(pl.program_id(1) == 0)
    def _():
        _cast_weight(w_ref, ws_ref)

    o_ref[...] = r_ref[...] + _dot(a_ref[...], ws_ref[...])


def _matmul_resid(a, w, layer, resid, *, tm, tn, name):
    t, k = a.shape
    n = w.shape[2]
    return pl.pallas_call(
        _matmul_resid_kernel,
        grid=(n // tn, t // tm),
        in_specs=[
            pl.BlockSpec((tm, k), lambda j, i: (i, 0)),
            pl.BlockSpec((1, k, tn), lambda j, i: (layer, 0, j)),
            pl.BlockSpec((tm, tn), lambda j, i: (i, j)),
        ],
        out_specs=pl.BlockSpec((tm, tn), lambda j, i: (i, j)),
        out_shape=jax.ShapeDtypeStruct((t, n), F32),
        scratch_shapes=[pltpu.VMEM((k, tn), BF16)],
        compiler_params=_cparams("parallel", "arbitrary"),
        name=name,
    )(a, w, resid)


def _ffn_gu_kernel(x_ref, g_ref, wg_ref, wu_ref, o_ref, h_ref):
    @pl.when(pl.program_id(1) == 0)
    def _():
        h_ref[...] = _rmsnorm_rows(x_ref[...], g_ref[...]).astype(BF16)

    h = h_ref[...]
    gate = _dot(h, wg_ref[0].astype(BF16))
    up = _dot(h, wu_ref[0].astype(BF16))
    o_ref[...] = (_silu(gate) * up).astype(o_ref.dtype)


def _ffn_gu(x, g, w_gu, layer, *, tm=1024, tf=512):
    t, d = x.shape
    f = w_gu.shape[2] // 2
    nf = f // tf
    return pl.pallas_call(
        _ffn_gu_kernel,
        grid=(t // tm, nf),
        in_specs=[
            pl.BlockSpec((tm, d), lambda i, j: (i, 0)),
            pl.BlockSpec((1, d), lambda i, j: (0, 0)),
            pl.BlockSpec((1, d, tf), lambda i, j: (layer, 0, j)),
            pl.BlockSpec((1, d, tf), lambda i, j: (layer, 0, nf + j)),
        ],
        out_specs=pl.BlockSpec((tm, tf), lambda i, j: (i, j)),
        out_shape=jax.ShapeDtypeStruct((t, f), BF16),
        scratch_shapes=[pltpu.VMEM((tm, d), BF16)],
        compiler_params=_cparams("parallel", "arbitrary"),
        name="ffn_gu",
    )(x, g, w_gu, w_gu)


def _pack_pairs(lo, hi):
    lo_bits = lax.bitcast_convert_type(lo.astype(BF16).astype(F32), jnp.uint32)
    hi_bits = lax.bitcast_convert_type(hi.astype(BF16).astype(F32), jnp.uint32)
    return (hi_bits & jnp.uint32(0xFFFF0000)) | (lo_bits >> 16)


def _unpack_pairs(packed):
    lo = lax.bitcast_convert_type(packed << 16, F32)
    hi = lax.bitcast_convert_type(packed & jnp.uint32(0xFFFF0000), F32)
    return lo, hi


def _router_kernel(x_ref, g_ref, wr_ref, br_ref, h_ref, r_ref):
    h = _rmsnorm_rows(x_ref[...], g_ref[...])
    half = h.shape[1] // 2
    h_ref[...] = _pack_pairs(h[:, :half], h[:, half:])
    h1, h2, h3 = _split3(h)
    w = wr_ref[...]
    w1, w2, w3 = _split3(w)
    logits = (_dot(h1, w1) + (_dot(h1, w2) + _dot(h2, w1))
              + (_dot(h1, w3) + _dot(h2, w2) + _dot(h3, w1))) + br_ref[...]
    lane = lax.broadcasted_iota(jnp.int32, logits.shape, 1)
    logits = jnp.where(lane < N_EXPERTS, logits, -jnp.inf)
    m1 = jnp.max(logits, axis=-1, keepdims=True)
    i1 = jnp.min(jnp.where(logits == m1, lane, SM_W), axis=-1, keepdims=True)
    rest = jnp.where(lane == i1, -jnp.inf, logits)
    m2 = jnp.max(rest, axis=-1, keepdims=True)
    i2 = jnp.min(jnp.where(rest == m2, lane, SM_W), axis=-1, keepdims=True)
    e = jnp.exp(m2 - m1)
    p1 = 1.0 / (1.0 + e)
    p2 = e / (1.0 + e)
    r_ref[...] = jnp.where(lane == 0, i1.astype(F32),
                           jnp.where(lane == 1, i2.astype(F32),
                                     jnp.where(lane == 2, p1, jnp.where(lane == 3, p2, 0.0))))


def _router(x, g, w_router, b_router, *, tm=512):
    t, d = x.shape
    return pl.pallas_call(
        _router_kernel,
        grid=(t // tm,),
        in_specs=[
            pl.BlockSpec((tm, d), lambda i: (i, 0)),
            pl.BlockSpec((1, d), lambda i: (0, 0)),
            pl.BlockSpec((d, SM_W), lambda i: (0, 0)),
            pl.BlockSpec((1, SM_W), lambda i: (0, 0)),
        ],
        out_specs=[
            pl.BlockSpec((tm, d // 2), lambda i: (i, 0)),
            pl.BlockSpec((tm, SM_W), lambda i: (i, 0)),
        ],
        out_shape=[
            jax.ShapeDtypeStruct((t, d // 2), jnp.uint32),
            jax.ShapeDtypeStruct((t, SM_W), F32),
        ],
        compiler_params=_cparams("parallel"),
        name="router",
    )(x, g, w_router, b_router)


def _row_copy(src_ref, dst_ref, sem, src_row, dst_row):
    return pltpu.make_async_copy(src_ref.at[pl.ds(src_row, 1), :], dst_ref.at[pl.ds(dst_row, 1), :], sem)


def _gather_kernel(idx_ref, src_ref, o_ref, buf_ref, sem):
    rows = buf_ref.shape[0]
    base = pl.program_id(0) * rows

    def start(blk, carry):
        for u in range(DMA_UNROLL):
            r = blk * DMA_UNROLL + u
            _row_copy(src_ref, buf_ref, sem, idx_ref[base + r], r).start(priority=u % 2)
        return carry

    lax.fori_loop(0, rows // DMA_UNROLL, start, 0)

    def wait(blk, carry):
        for u in range(DMA_UNROLL):
            _row_copy(src_ref, buf_ref, sem, 0, blk * DMA_UNROLL + u).wait()
        return carry

    lax.fori_loop(0, rows // DMA_UNROLL, wait, 0)
    lo, hi = _unpack_pairs(buf_ref[...])
    half = buf_ref.shape[1]
    o_ref[:, :half] = lo.astype(o_ref.dtype)
    o_ref[:, half:] = hi.astype(o_ref.dtype)


def _gather_rows(src, idx, *, rows=256):
    p = idx.shape[0]
    half = src.shape[1]
    d = 2 * half
    return pl.pallas_call(
        _gather_kernel,
        grid_spec=pltpu.PrefetchScalarGridSpec(
            num_scalar_prefetch=1,
            grid=(p // rows,),
            in_specs=[pl.BlockSpec(memory_space=pl.ANY)],
            out_specs=pl.BlockSpec((rows, d), lambda i, idx_ref: (i, 0)),
            scratch_shapes=[pltpu.VMEM((rows, half), src.dtype), pltpu.SemaphoreType.DMA(())],
        ),
        out_shape=jax.ShapeDtypeStruct((p, d), BF16),
        compiler_params=_cparams("arbitrary"),
        name="moe_gather",
    )(idx, src)


def _expert_changed(te_ref):
    i = pl.program_id(1)
    return jnp.logical_or(i == 0, te_ref[i] != te_ref[jnp.maximum(i - 1, 0)])


def _moe_gu_kernel(te_ref, nu_ref, a_ref, wg_ref, wu_ref, o_ref, wgs_ref, wus_ref):
    used = pl.program_id(1) < nu_ref[0]

    @pl.when(_expert_changed(te_ref))
    def _():
        _cast_weight(wg_ref, wgs_ref)
        _cast_weight(wu_ref, wus_ref)

    @pl.when(used)
    def _():
        a = a_ref[...]
        o_ref[...] = (_silu(_dot(a, wgs_ref[...])) * _dot(a, wus_ref[...])).astype(o_ref.dtype)

    @pl.when(jnp.logical_not(used))
    def _():
        o_ref[...] = jnp.zeros_like(o_ref)


def _moe_gu(a, w_gu, layer, tile_expert, n_used, *, tm, tf=1024):
    p, d = a.shape
    f = w_gu.shape[3] // 2
    nf = f // tf
    return pl.pallas_call(
        _moe_gu_kernel,
        grid_spec=pltpu.PrefetchScalarGridSpec(
            num_scalar_prefetch=2,
            grid=(nf, p // tm),
            in_specs=[
                pl.BlockSpec((tm, d), lambda j, i, te, nu: (i, 0)),
                pl.BlockSpec((1, 1, d, tf), lambda j, i, te, nu: (layer, te[i], 0, j)),
                pl.BlockSpec((1, 1, d, tf), lambda j, i, te, nu: (layer, te[i], 0, nf + j)),
            ],
            out_specs=pl.BlockSpec((tm, tf), lambda j, i, te, nu: (i, j)),
            scratch_shapes=[pltpu.VMEM((d, tf), BF16), pltpu.VMEM((d, tf), BF16)],
        ),
        out_shape=jax.ShapeDtypeStruct((p, f), BF16),
        compiler_params=_cparams("parallel", "arbitrary"),
        name="moe_gu",
    )(tile_expert, n_used, a, w_gu, w_gu)


def _moe_down_kernel(te_ref, nu_ref, a_ref, w_ref, o_ref, ws_ref):
    used = pl.program_id(1) < nu_ref[0]

    @pl.when(_expert_changed(te_ref))
    def _():
        _cast_weight(w_ref, ws_ref)

    @pl.when(used)
    def _():
        y = _dot(a_ref[...], ws_ref[...])
        half = y.shape[1] // 2
        o_ref[...] = _pack_pairs(y[:, :half], y[:, half:])

    @pl.when(jnp.logical_not(used))
    def _():
        o_ref[...] = jnp.zeros_like(o_ref)


def _moe_down(a, w_down, layer, tile_expert, n_used, *, tm, tn=MOE_DOWN_TN):
    p, f = a.shape
    n = w_down.shape[3]
    return pl.pallas_call(
        _moe_down_kernel,
        grid_spec=pltpu.PrefetchScalarGridSpec(
            num_scalar_prefetch=2,
            grid=(n // tn, p // tm),
            in_specs=[
                pl.BlockSpec((tm, f), lambda j, i, te, nu: (i, 0)),
                pl.BlockSpec((1, 1, f, tn), lambda j, i, te, nu: (layer, te[i], 0, j)),
            ],
            out_specs=pl.BlockSpec((tm, tn // 2), lambda j, i, te, nu: (i, j)),
            scratch_shapes=[pltpu.VMEM((f, tn), BF16)],
        ),
        out_shape=jax.ShapeDtypeStruct((p, n // 2), jnp.uint32),
        compiler_params=_cparams("parallel", "arbitrary"),
        name="moe_down",
    )(tile_expert, n_used, a, w_down)


def _combine_kernel(p1_ref, p2_ref, y_ref, x_ref, r_ref, o_ref, b1_ref, b2_ref, sem):
    rows = b1_ref.shape[0]
    base = pl.program_id(0) * rows

    def start(blk, carry):
        for u in range(DMA_UNROLL):
            r = blk * DMA_UNROLL + u
            _row_copy(y_ref, b1_ref, sem.at[0], p1_ref[base + r], r).start(priority=0)
            _row_copy(y_ref, b2_ref, sem.at[1], p2_ref[base + r], r).start(priority=1)
        return carry

    lax.fori_loop(0, rows // DMA_UNROLL, start, 0)

    def wait(blk, carry):
        for u in range(DMA_UNROLL):
            r = blk * DMA_UNROLL + u
            _row_copy(y_ref, b1_ref, sem.at[0], 0, r).wait()
            _row_copy(y_ref, b2_ref, sem.at[1], 0, r).wait()
        return carry

    lax.fori_loop(0, rows // DMA_UNROLL, wait, 0)
    route = r_ref[...]
    w1 = route[:, 2:3]
    w2 = route[:, 3:4]
    lo1, hi1 = _unpack_pairs(b1_ref[...])
    lo2, hi2 = _unpack_pairs(b2_ref[...])
    lo = w1 * lo1 + w2 * lo2
    hi = w1 * hi1 + w2 * hi2
    hw = MOE_DOWN_TN // 2
    for j in range(lo.shape[1] // hw):
        c = j * MOE_DOWN_TN
        o_ref[:, c:c + hw] = x_ref[:, c:c + hw] + lo[:, j * hw:(j + 1) * hw]
        o_ref[:, c + hw:c + 2 * hw] = x_ref[:, c + hw:c + 2 * hw] + hi[:, j * hw:(j + 1) * hw]


def _combine(y_sorted, x, route, pos1, pos2, *, rows=256):
    t, d = x.shape
    half = y_sorted.shape[1]
    return pl.pallas_call(
        _combine_kernel,
        grid_spec=pltpu.PrefetchScalarGridSpec(
            num_scalar_prefetch=2,
            grid=(t // rows,),
            in_specs=[
                pl.BlockSpec(memory_space=pl.ANY),
                pl.BlockSpec((rows, d), lambda i, p1, p2: (i, 0)),
                pl.BlockSpec((rows, SM_W), lambda i, p1, p2: (i, 0)),
            ],
            out_specs=pl.BlockSpec((rows, d), lambda i, p1, p2: (i, 0)),
            scratch_shapes=[pltpu.VMEM((rows, half), jnp.uint32), pltpu.VMEM((rows, half), jnp.uint32),
                            pltpu.SemaphoreType.DMA((2,))],
        ),
        out_shape=jax.ShapeDtypeStruct((t, d), F32),
        compiler_params=_cparams("arbitrary"),
        name="moe_combine",
    )(pos1, pos2, y_sorted, x, route)


MOE_TM = 512


def _moe_plan(route, tm):
    t = route.shape[0]
    n_tiles = (t * TOP_K) // tm + N_EXPERTS
    eid = route[:, :TOP_K].astype(jnp.int32).reshape(-1)
    onehot = (eid[:, None] == jnp.arange(N_EXPERTS, dtype=jnp.int32)[None, :]).astype(jnp.int32)
    csum = jnp.cumsum(onehot, axis=0)
    rank = jnp.sum((csum - onehot) * onehot, axis=1)
    counts = csum[-1]
    tiles_per = (counts + tm - 1) // tm
    tile_end = jnp.cumsum(tiles_per)
    start = (tile_end - tiles_per) * tm
    pos = (jnp.sum(onehot * start[None, :], axis=1) + rank).astype(jnp.int32)
    n_used = tile_end[-1].astype(jnp.int32)
    tile_id = jnp.arange(n_tiles, dtype=jnp.int32)
    tile_expert = jnp.sum((tile_id[:, None] >= tile_end[None, :]).astype(jnp.int32), axis=1)
    tile_expert = jnp.minimum(tile_expert, N_EXPERTS - 1).astype(jnp.int32)
    last_expert = jnp.max(jnp.where(counts > 0, jnp.arange(N_EXPERTS, dtype=jnp.int32), 0))
    tile_expert = jnp.where(tile_id < n_used, tile_expert, last_expert).astype(jnp.int32)
    token = jnp.repeat(jnp.arange(t, dtype=jnp.int32), TOP_K)
    row_token = jnp.zeros((n_tiles * tm,), jnp.int32).at[pos].set(token)
    pos = pos.reshape(t, TOP_K)
    return row_token, tile_expert, n_used.reshape(1), pos[:, 0], pos[:, 1]


def _moe_ffn(x, g, w_router, b_router, w_gu, w_down, layer):
    h, route = _router(x, g, w_router, b_router)
    row_token, tile_expert, n_used, pos1, pos2 = _moe_plan(route, MOE_TM)
    a = _gather_rows(h, row_token)
    act = _moe_gu(a, w_gu, layer, tile_expert, n_used, tm=MOE_TM)
    y = _moe_down(act, w_down, layer, tile_expert, n_used, tm=MOE_TM)
    return _combine(y, x, route, pos1, pos2)


def _final_norm_kernel(x_ref, g_ref, o_ref):
    o_ref[...] = _rmsnorm_rows(x_ref[...], g_ref[...])


def _final_norm(x, g, *, tm=512):
    t, d = x.shape
    return pl.pallas_call(
        _final_norm_kernel,
        grid=(t // tm,),
        in_specs=[pl.BlockSpec((tm, d), lambda i: (i, 0)), pl.BlockSpec((1, d), lambda i: (0, 0))],
        out_specs=pl.BlockSpec((tm, d), lambda i: (i, 0)),
        out_shape=jax.ShapeDtypeStruct((t, d), F32),
        compiler_params=_cparams("parallel"),
        name="final_norm",
    )(x, g)


def _lane_row(vec, offset):
    return jnp.zeros((1, SM_W), F32).at[0, offset:offset + vec.shape[0]].set(vec.astype(F32))


def _mixer(x, l, bsz, seq, p, consts):
    tril, triu, expand = consts
    h, small, small_t = _norm_small(x, p["norm_mix_g"][l][None, :], p["w_in_t"], l)
    conv_w = jnp.zeros((CONV_K, PC_TOTAL), F32)
    conv_w = conv_w.at[:, PC_Q:PC_V].set(p["qk_conv_w"][l]).at[:, PC_XBC:PC_GATE].set(p["ssd_conv_w"][l])
    conv_b = jnp.zeros((1, PC_TOTAL), F32)
    conv_b = conv_b.at[0, PC_Q:PC_V].set(p["qk_conv_b"][l]).at[0, PC_XBC:PC_GATE].set(p["ssd_conv_b"][l])
    proj = _proj(h, p["w_in_t"], l, conv_w, conv_b, seq=seq)

    if_bias = p["if_bias"][l]
    y_a = _mlstm(proj, small, small_t,
                 _lane_row(if_bias, 0),
                 jnp.broadcast_to(if_bias.astype(F32)[:, None], (2 * ML_HEADS, SEQ_CHUNK)),
                 p["mh_norm_g"][l][None, :], tril, triu, bsz=bsz, seq=seq)
    y_b = _ssd(proj, small, small_t,
               _lane_row(p["dt_bias"][l], SM_DT),
               jnp.broadcast_to(p["dt_bias"][l].astype(F32)[:, None], (SSD_HEADS, SEQ_CHUNK)),
               _lane_row(p["a_log"][l], SM_DT),
               jnp.broadcast_to(p["a_log"][l].astype(F32)[:, None], (SSD_HEADS, SEQ_CHUNK)),
               jnp.repeat(p["d_skip"][l].astype(F32), SSD_P)[None, :],
               p["ssd_norm_g"][l][None, :], tril, triu, expand, bsz=bsz, seq=seq)
    merged = _merge(y_a, y_b, p["w_branch_a"], p["w_branch_b"], l, proj, p["gate_bias"][l][None, :])
    return _matmul_resid(merged, p["w_out"], l, x, tm=1024, tn=1024, name="out_proj")


def kernel(x, norm_mix_g, w_in, if_bias, qk_conv_w, qk_conv_b, mh_norm_g, ssd_conv_w, ssd_conv_b,
           dt_bias, a_log, d_skip, ssd_norm_g, gate_bias, w_branch_a, w_branch_b, w_out, norm_ffn_g,
           ffn_w_gu, ffn_w_down, router_w, router_b, exp_w_gu, exp_w_down, norm_final_g):
    p = dict(norm_mix_g=norm_mix_g, w_in_t=jnp.swapaxes(w_in, 1, 2), if_bias=if_bias, qk_conv_w=qk_conv_w,
             qk_conv_b=qk_conv_b, mh_norm_g=mh_norm_g, ssd_conv_w=ssd_conv_w, ssd_conv_b=ssd_conv_b,
             dt_bias=dt_bias, a_log=a_log, d_skip=d_skip, ssd_norm_g=ssd_norm_g, gate_bias=gate_bias,
             w_branch_a=w_branch_a, w_branch_b=w_branch_b, w_out=w_out)
    bsz, seq, d = x.shape
    assert d == D_MODEL and seq % SEQ_CHUNK == 0 and (bsz * seq) % 2048 == 0, x.shape
    x = x.reshape(bsz * seq, d)

    idx = jnp.arange(SEQ_CHUNK, dtype=jnp.int32)
    tril = (idx[:, None] >= idx[None, :]).astype(BF16)
    triu = (idx[:, None] <= idx[None, :]).astype(BF16)
    lane = jnp.arange(SM_W, dtype=jnp.int32)[:, None]
    chan = jnp.arange(SSD_WIDTH, dtype=jnp.int32)[None, :]
    expand = (lane == SM_DT + chan // SSD_P).astype(BF16)
    consts = (tril, triu, expand)

    for l in range(DEPTH):
        x = _mixer(x, l, bsz, seq, p, consts)
        g = norm_ffn_g[l][None, :]
        if l % 2 == 0:
            act = _ffn_gu(x, g, ffn_w_gu, l // 2)
            x = _matmul_resid(act, ffn_w_down, l // 2, x, tm=512, tn=512, name="ffn_down")
        else:
            w_r = jnp.pad(router_w[l // 2], ((0, 0), (0, SM_W - N_EXPERTS)))
            b_r = _lane_row(router_b[l // 2], 0)
            x = _moe_ffn(x, g, w_r, b_r, exp_w_gu, exp_w_down, l // 2)
    return _final_norm(x, norm_final_g[None, :]).reshape(bsz, seq, d)
```

```python
import functools
import math

import jax
import jax.numpy as jnp
from jax import lax
from jax.experimental import pallas as pl
from jax.experimental.pallas import tpu as pltpu

F32 = jnp.float32
BF16 = jnp.bfloat16

D_MODEL = 2048
DEPTH = 4
EPS = 1e-6
CONV_K = 4
ML_HEADS = 4
ML_DV = 512
ML_DK = 256
ML_QK = ML_HEADS * ML_DK
ML_WIDTH = ML_HEADS * ML_DV
SSD_HEADS = 32
SSD_P = 64
SSD_GROUPS = 8
SSD_N = 128
SSD_R = SSD_HEADS // SSD_GROUPS
SSD_WIDTH = SSD_HEADS * SSD_P
SSD_GN = SSD_GROUPS * SSD_N
SSD_CONV = SSD_WIDTH + 2 * SSD_GN
SSD_GW = SSD_R * SSD_P
FFN_DENSE = 5632
N_EXPERTS = 8
TOP_K = 2
FFN_EXPERT = 4096

PC_Q = 0
PC_K = 1024
PC_V = 2048
PC_O = 4096
PC_Z = 6144
PC_XBC = 8192
PC_GATE = 12288
PC_TOTAL = 16384
SRC_IF = 6144
SRC_DT = 12296
SHIFT_ZXBC = 8
SHIFT_GATE = 40
PROJ_TN = 1024
SM_W = 128
SM_COLS = 2 * SM_W
SM_DT = SRC_DT - (SRC_DT // SM_W) * SM_W

V7X_VMEM_BYTES = 64 * 1024 * 1024
VMEM_LIMIT = 58 * 1024 * 1024

DMA_UNROLL = 16
MOE_DOWN_TN = 1024
SEQ_CHUNK = 256
CARRY = 8


def _cparams(*sem):
    return pltpu.CompilerParams(dimension_semantics=sem, vmem_limit_bytes=VMEM_LIMIT)


def _sigmoid(x):
    return 1.0 / (1.0 + jnp.exp(-x))


def _silu(x):
    half = 0.5 * x
    return half * jnp.tanh(half) + half


def _softplus(x):
    return jnp.maximum(x, 0.0) + jnp.log(1.0 + jnp.exp(-jnp.abs(x)))


def _log_sigmoid(x):
    return jnp.minimum(x, 0.0) - jnp.log(1.0 + jnp.exp(-jnp.abs(x)))


def _split3(x):
    x1 = x.astype(BF16)
    r1 = x - x1.astype(F32)
    x2 = r1.astype(BF16)
    x3 = (r1 - x2.astype(F32)).astype(BF16)
    return x1, x2, x3


def _dot(a, b):
    return jnp.dot(a, b, preferred_element_type=F32)


def _dot_nt(a, b):
    return lax.dot_general(a, b, (((1,), (1,)), ((), ())), preferred_element_type=F32)


def _dot_tn(a, b):
    return lax.dot_general(a, b, (((0,), (0,)), ((), ())), preferred_element_type=F32)


def _dot_f32_left(x, m01):
    x1, x2, x3 = _split3(x)
    return _dot(x1, m01) + _dot(x2, m01) + _dot(x3, m01)


def _dot_f32_right(m01, x):
    x1, x2, x3 = _split3(x)
    return _dot(m01, x1) + _dot(m01, x2) + _dot(m01, x3)


def _rmsnorm_rows(x, g):
    ms = jnp.mean(x * x, axis=-1, keepdims=True)
    return (x * lax.rsqrt(ms + EPS)) * g


def _norm_small_kernel(x_ref, g_ref, w0_ref, w1_ref, h_ref, s_ref, st_ref):
    hb = _rmsnorm_rows(x_ref[...], g_ref[...]).astype(BF16)
    h_ref[...] = hb
    wt = jnp.concatenate([w0_ref[0], w1_ref[0]], axis=0).astype(BF16)
    s_ref[...] = _dot_nt(hb, wt)
    st_ref[...] = _dot_nt(wt, hb)


def _norm_small(x, g, w_in_t, layer, *, tm=1024):
    t, d = x.shape
    return pl.pallas_call(
        _norm_small_kernel,
        grid=(t // tm,),
        in_specs=[
            pl.BlockSpec((tm, d), lambda i: (i, 0)),
            pl.BlockSpec((1, d), lambda i: (0, 0)),
            pl.BlockSpec((1, SM_W, d), lambda i: (layer, SRC_IF // SM_W, 0)),
            pl.BlockSpec((1, SM_W, d), lambda i: (layer, SRC_DT // SM_W, 0)),
        ],
        out_specs=[
            pl.BlockSpec((tm, d), lambda i: (i, 0)),
            pl.BlockSpec((tm, SM_COLS), lambda i: (i, 0)),
            pl.BlockSpec((SM_COLS, tm), lambda i: (0, i)),
        ],
        out_shape=[
            jax.ShapeDtypeStruct((t, d), BF16),
            jax.ShapeDtypeStruct((t, SM_COLS), F32),
            jax.ShapeDtypeStruct((SM_COLS, t), F32),
        ],
        compiler_params=_cparams("parallel"),
        name="norm_small",
    )(x, g, w_in_t, w_in_t)


CAST_ROWS = 256


def _conv_silu(acc, carry, cw_ref, cb_ref):
    cat = jnp.concatenate([carry, acc], axis=0)
    y = cb_ref[...] + cw_ref[CONV_K - 1:CONV_K, :] * acc
    for back in range(1, CONV_K):
        tap = pltpu.roll(cat, back, axis=0)[CARRY:, :]
        y = y + cw_ref[CONV_K - 1 - back:CONV_K - back, :] * tap
    return _silu(y)


PROJ_SUB_ROWS = 256


def _proj_kernel(h_ref, wt_ref, cw_ref, cb_ref, o_ref, ws_ref, carry_ref, acc_ref, *, tiles_per_seq):
    j = pl.program_id(0)
    i = pl.program_id(1)
    tn = ws_ref.shape[1]

    @pl.when(i == 0)
    def _():
        for r in range(0, tn, CAST_ROWS):
            ws_ref[:, r:r + CAST_ROWS] = wt_ref[0, r:r + CAST_ROWS, :].T.astype(BF16)

    @pl.when(i % tiles_per_seq == 0)
    def _():
        carry_ref[...] = jnp.zeros_like(carry_ref)

    is_conv = jnp.logical_or(j < PC_V // tn, jnp.logical_and(j >= PC_XBC // tn, j < PC_GATE // tn))
    is_sigmoid = jnp.logical_and(j >= PC_O // tn, j < PC_Z // tn)
    is_silu = jnp.logical_and(j >= PC_Z // tn, j < PC_XBC // tn)
    is_plain = jnp.logical_not(jnp.logical_or(is_conv, jnp.logical_or(is_sigmoid, is_silu)))

    sub = PROJ_SUB_ROWS
    row_blocks = [slice(r, r + sub) for r in range(0, h_ref.shape[0], sub)]

    def sub_dot(s):
        acc_ref[s % 2] = _dot(h_ref[row_blocks[s], :], ws_ref[...])

    def pipelined(epilogue):
        sub_dot(0)
        for s, rows in enumerate(row_blocks):
            if s + 1 < len(row_blocks):
                sub_dot(s + 1)
            o_ref[rows, :] = epilogue(acc_ref[s % 2]).astype(o_ref.dtype)

    @pl.when(is_conv)
    def _():
        carry = [carry_ref[...]]

        def conv(acc):
            y = _conv_silu(acc, carry[0], cw_ref, cb_ref)
            carry[0] = acc[sub - CARRY:, :]
            return y

        pipelined(conv)
        carry_ref[...] = carry[0]

    @pl.when(is_sigmoid)
    def _():
        pipelined(_sigmoid)

    @pl.when(is_silu)
    def _():
        pipelined(_silu)

    @pl.when(is_plain)
    def _():
        for rows in row_blocks:
            o_ref[rows, :] = _dot(h_ref[rows, :], ws_ref[...]).astype(o_ref.dtype)


def _proj_src_row(j, tn):
    past_z = (j >= PC_Z // tn).astype(jnp.int32)
    past_gate = (j >= PC_GATE // tn).astype(jnp.int32)
    shift8 = past_z * (SHIFT_ZXBC // 8) + past_gate * ((SHIFT_GATE - SHIFT_ZXBC) // 8)
    return pl.multiple_of(j * tn + 8 * shift8, 8)


def _proj(h, w_in_t, layer, conv_w, conv_b, *, seq, tm=2048):
    t, d = h.shape
    tn = PROJ_TN
    assert t % tm == 0 and seq % tm == 0, (t, seq, tm)
    return pl.pallas_call(
        functools.partial(_proj_kernel, tiles_per_seq=seq // tm),
        grid=(PC_TOTAL // tn, t // tm),
        in_specs=[
            pl.BlockSpec((tm, d), lambda j, i: (i, 0)),
            pl.BlockSpec((pl.Element(1), pl.Element(tn), pl.Element(d)),
                         lambda j, i: (layer, _proj_src_row(j, tn), 0)),
            pl.BlockSpec((CONV_K, tn), lambda j, i: (0, j)),
            pl.BlockSpec((1, tn), lambda j, i: (0, j)),
        ],
        out_specs=pl.BlockSpec((tm, tn), lambda j, i: (i, j)),
        out_shape=jax.ShapeDtypeStruct((t, PC_TOTAL), BF16),
        scratch_shapes=[pltpu.VMEM((d, tn), BF16), pltpu.VMEM((CARRY, tn), F32),
                        pltpu.VMEM((2, PROJ_SUB_ROWS, tn), F32)],
        compiler_params=_cparams("parallel", "arbitrary"),
        name="proj",
    )(h, w_in_t, conv_w, conv_b)


def _mlstm_kernel(q_ref, k_ref, v_ref, o_ref, sm_ref, smt_ref, ifr_ref, ifc_ref,
                  g_ref, tril_ref, triu_ref, y_ref, c_ref, n_ref, m_ref):
    length = q_ref.shape[0]

    @pl.when(pl.program_id(1) == 0)
    def _():
        c_ref[...] = jnp.zeros_like(c_ref)
        n_ref[...] = jnp.zeros_like(n_ref)
        m_ref[...] = jnp.zeros_like(m_ref)

    cols = sm_ref[...] + ifr_ref[...]
    rows = smt_ref[...] + ifc_ref[...]
    b_cols = _dot_f32_right(tril_ref[...], _log_sigmoid(cols))
    b_rows = _dot_f32_left(_log_sigmoid(rows), triu_ref[...])
    r_i = lax.broadcasted_iota(jnp.int32, (length, length), 0)
    c_i = lax.broadcasted_iota(jnp.int32, (length, length), 1)
    causal = r_i >= c_i

    for h in range(ML_HEADS):
        kb = k_ref[:, h * ML_DK:(h + 1) * ML_DK]
        q_h = q_ref[:, h * ML_DK:(h + 1) * ML_DK].astype(F32) * (ML_DK ** -0.5)
        k_h = kb.astype(F32)
        qb = q_h.astype(BF16)
        v_h = v_ref[:, h * ML_DV:(h + 1) * ML_DV]
        i_col = cols[:, h:h + 1]
        b_col = b_cols[:, ML_HEADS + h:ML_HEADS + h + 1]
        i_row = rows[h:h + 1, :]
        b_row = b_rows[ML_HEADS + h:ML_HEADS + h + 1, :]
        m_prev = m_ref[h:h + 1, 0:1]
        c_prev = c_ref[h]
        n_prev = n_ref[h:h + 1, :]

        log_d = jnp.where(causal, b_col - b_row + i_row, -jnp.inf)
        m_inter = b_col + m_prev
        m_t = jnp.maximum(m_inter, jnp.max(log_d, axis=-1, keepdims=True))
        scores = _dot_nt(qb, kb) * jnp.exp(log_d - m_t)
        inter = jnp.exp(m_inter - m_t)
        num = _dot(scores.astype(BF16), v_h) + inter * _dot(qb, c_prev.astype(BF16))
        den = (jnp.sum(scores, axis=-1, keepdims=True)
               + inter * jnp.sum(q_h * n_prev, axis=-1, keepdims=True))
        hh = num / jnp.maximum(jnp.abs(den), jnp.exp(-m_t))
        hh = hh * lax.rsqrt(jnp.mean(hh * hh, axis=-1, keepdims=True) + EPS)
        gate = o_ref[:, h * ML_DV:(h + 1) * ML_DV].astype(F32)
        y_ref[:, h * ML_DV:(h + 1) * ML_DV] = (
            hh * g_ref[:, h * ML_DV:(h + 1) * ML_DV] * gate).astype(y_ref.dtype)

        b_last = b_col[length - 1:length, :]
        log_w_row = b_last - b_row + i_row
        m_new = jnp.maximum(b_last + m_prev, jnp.max(log_w_row, axis=-1, keepdims=True))
        w_col = jnp.exp(b_last - b_col + i_col - m_new)
        decay = jnp.exp(b_last + m_prev - m_new)
        vw = (v_h.astype(F32) * w_col).astype(BF16)
        c_ref[h] = decay * c_prev + _dot_tn(kb, vw)
        n_ref[h:h + 1, :] = decay * n_prev + jnp.sum(k_h * w_col, axis=0, keepdims=True)
        m_ref[h:h + 1, :] = jnp.broadcast_to(m_new, (1, m_ref.shape[1]))


def _mlstm(proj, small, small_t, if_row, if_col, norm_g, tril, triu, *, bsz, seq):
    length = SEQ_CHUNK
    nc = seq // length
    t = bsz * seq
    row = lambda b, c: b * nc + c
    return pl.pallas_call(
        _mlstm_kernel,
        grid=(bsz, nc),
        in_specs=[
            pl.BlockSpec((length, ML_QK), lambda b, c: (row(b, c), PC_Q // ML_QK)),
            pl.BlockSpec((length, ML_QK), lambda b, c: (row(b, c), PC_K // ML_QK)),
            pl.BlockSpec((length, ML_WIDTH), lambda b, c: (row(b, c), PC_V // ML_WIDTH)),
            pl.BlockSpec((length, ML_WIDTH), lambda b, c: (row(b, c), PC_O // ML_WIDTH)),
            pl.BlockSpec((length, SM_W), lambda b, c: (row(b, c), 0)),
            pl.BlockSpec((2 * ML_HEADS, length), lambda b, c: (0, row(b, c))),
            pl.BlockSpec((1, SM_W), lambda b, c: (0, 0)),
            pl.BlockSpec((2 * ML_HEADS, length), lambda b, c: (0, 0)),
            pl.BlockSpec((1, ML_WIDTH), lambda b, c: (0, 0)),
            pl.BlockSpec((length, length), lambda b, c: (0, 0)),
            pl.BlockSpec((length, length), lambda b, c: (0, 0)),
        ],
        out_specs=pl.BlockSpec((length, ML_WIDTH), lambda b, c: (row(b, c), 0)),
        out_shape=jax.ShapeDtypeStruct((t, ML_WIDTH), BF16),
        scratch_shapes=[
            pltpu.VMEM((ML_HEADS, ML_DK, ML_DV), F32),
            pltpu.VMEM((2 * ML_HEADS, ML_DK), F32),
            pltpu.VMEM((2 * ML_HEADS, 128), F32),
        ],
        compiler_params=_cparams("parallel", "arbitrary"),
        name="mlstm",
    )(proj, proj, proj, proj, small, small_t, if_row, if_col, norm_g, tril, triu)


def _ssd_kernel(xbc_ref, z_ref, sm_ref, smt_ref, dtb_r_ref, dtb_c_ref, al_r_ref,
                al_c_ref, dsk_ref, g_ref, tril_ref, triu_ref, e_ref, y_ref, st_ref):
    length = xbc_ref.shape[0]

    @pl.when(pl.program_id(1) == 0)
    def _():
        st_ref[...] = jnp.zeros_like(st_ref)

    lane = lax.broadcasted_iota(jnp.int32, (1, SM_W), 1)
    head_lane = jnp.logical_and(lane >= SM_DT, lane < SM_DT + SSD_HEADS)
    dt_c = jnp.where(head_lane, _softplus(sm_ref[...] + dtb_r_ref[...]), 0.0)
    dt_r = _softplus(smt_ref[SM_DT:SM_DT + SSD_HEADS, :] + dtb_c_ref[...])
    a_c = dt_c * (-jnp.exp(al_r_ref[...]))
    a_r = dt_r * (-jnp.exp(al_c_ref[...]))
    acum_c = _dot_f32_right(tril_ref[...], a_c)
    acum_r = _dot_f32_left(a_r, triu_ref[...])
    a_last = acum_c[length - 1:length, :]

    e01 = e_ref[...]

    def expand(x):
        x1 = x.astype(BF16)
        x2 = (x - x1.astype(F32)).astype(BF16)
        return _dot(x1, e01) + _dot(x2, e01)

    dt_e = expand(dt_c)
    from_start_e = expand(jnp.exp(acum_c))
    to_end_e = expand(jnp.exp(a_last - acum_c))
    chunk_e = expand(jnp.broadcast_to(jnp.exp(a_last), (8, SM_W)))[0:1, :]

    r_i = lax.broadcasted_iota(jnp.int32, (length, length), 0)
    c_i = lax.broadcasted_iota(jnp.int32, (length, length), 1)
    causal = r_i >= c_i

    for g in range(SSD_GROUPS):
        ch = slice(g * SSD_GW, (g + 1) * SSD_GW)
        xs_g = xbc_ref[:, ch].astype(F32)
        bm = xbc_ref[:, SSD_WIDTH + g * SSD_N:SSD_WIDTH + (g + 1) * SSD_N]
        cm = xbc_ref[:, SSD_WIDTH + SSD_GN + g * SSD_N:SSD_WIDTH + SSD_GN + (g + 1) * SSD_N]
        xdt = xs_g * dt_e[:, ch]
        cb = _dot_nt(cm, bm)
        prev = st_ref[g]
        y = _dot(cm, prev.astype(BF16)) * from_start_e[:, ch] + dsk_ref[:, ch] * xs_g
        diag = []
        for r in range(SSD_R):
            h = g * SSD_R + r
            seg = acum_c[:, SM_DT + h:SM_DT + h + 1] - acum_r[h:h + 1, :]
            att = cb * jnp.exp(jnp.where(causal, seg, -jnp.inf))
            diag.append(_dot(att.astype(BF16), xdt[:, r * SSD_P:(r + 1) * SSD_P].astype(BF16)))
        y = y + jnp.concatenate(diag, axis=-1)
        st_ref[g] = prev * chunk_e[:, ch] + _dot_tn(bm, (xdt * to_end_e[:, ch]).astype(BF16))
        y = y * z_ref[:, ch].astype(F32)
        y = y * lax.rsqrt(jnp.mean(y * y, axis=-1, keepdims=True) + EPS)
        y_ref[:, ch] = (y * g_ref[:, ch]).astype(y_ref.dtype)


def _ssd(proj, small, small_t, dtb_row, dtb_col, al_row, al_col, dskip_e, norm_g,
         tril, triu, expand, *, bsz, seq):
    length = SEQ_CHUNK
    nc = seq // length
    t = bsz * seq
    row = lambda b, c: b * nc + c
    const = lambda b, c: (0, 0)
    return pl.pallas_call(
        _ssd_kernel,
        grid=(bsz, nc),
        in_specs=[
            pl.BlockSpec((length, SSD_CONV), lambda b, c: (row(b, c), PC_XBC // SSD_CONV)),
            pl.BlockSpec((length, SSD_WIDTH), lambda b, c: (row(b, c), PC_Z // SSD_WIDTH)),
            pl.BlockSpec((length, SM_W), lambda b, c: (row(b, c), 1)),
            pl.BlockSpec((2 * SSD_HEADS, length), lambda b, c: (SM_W // (2 * SSD_HEADS), row(b, c))),
            pl.BlockSpec((1, SM_W), const),
            pl.BlockSpec((SSD_HEADS, length), const),
            pl.BlockSpec((1, SM_W), const),
            pl.BlockSpec((SSD_HEADS, length), const),
            pl.BlockSpec((1, SSD_WIDTH), const),
            pl.BlockSpec((1, SSD_WIDTH), const),
            pl.BlockSpec((length, length), const),
            pl.BlockSpec((length, length), const),
            pl.BlockSpec((SM_W, SSD_WIDTH), const),
        ],
        out_specs=pl.BlockSpec((length, SSD_WIDTH), lambda b, c: (row(b, c), 0)),
        out_shape=jax.ShapeDtypeStruct((t, SSD_WIDTH), BF16),
        scratch_shapes=[pltpu.VMEM((SSD_GROUPS, SSD_N, SSD_GW), F32)],
        compiler_params=_cparams("parallel", "arbitrary"),
        name="ssd",
    )(proj, proj, small, small_t, dtb_row, dtb_col, al_row, al_col, dskip_e, norm_g, tril, triu, expand)


def _cast_weight(w_ref, ws_ref):
    lead = (0,) * (len(w_ref.shape) - 2)
    for r in range(0, ws_ref.shape[0], CAST_ROWS):
        ws_ref[r:r + CAST_ROWS, :] = w_ref[lead + (slice(r, r + CAST_ROWS), slice(None))].astype(BF16)


def _merge_kernel(ya_ref, yb_ref, wa_ref, wb_ref, ga_ref, gb_ref, ba_ref, bb_ref, o_ref, was_ref, wbs_ref):
    @pl.when(pl.program_id(1) == 0)
    def _():
        _cast_weight(wa_ref, was_ref)
        _cast_weight(wb_ref, wbs_ref)

    ga = _sigmoid(ga_ref[...].astype(F32) + ba_ref[...])
    gb = _sigmoid(gb_ref[...].astype(F32) + bb_ref[...])
    o_ref[...] = (ga * _dot(ya_ref[...], was_ref[...])
                  + gb * _dot(yb_ref[...], wbs_ref[...])).astype(o_ref.dtype)


def _merge(ya, yb, wa, wb, layer, proj, gate_bias, *, tm=1024, tn=512):
    t, d = ya.shape
    n = wa.shape[2]
    ga0 = PC_GATE // tn
    gb0 = (PC_GATE + n) // tn
    return pl.pallas_call(
        _merge_kernel,
        grid=(n // tn, t // tm),
        in_specs=[
            pl.BlockSpec((tm, d), lambda j, i: (i, 0)),
            pl.BlockSpec((tm, d), lambda j, i: (i, 0)),
            pl.BlockSpec((1, d, tn), lambda j, i: (layer, 0, j)),
            pl.BlockSpec((1, d, tn), lambda j, i: (layer, 0, j)),
            pl.BlockSpec((tm, tn), lambda j, i: (i, ga0 + j)),
            pl.BlockSpec((tm, tn), lambda j, i: (i, gb0 + j)),
            pl.BlockSpec((1, tn), lambda j, i: (0, j)),
            pl.BlockSpec((1, tn), lambda j, i: (0, n // tn + j)),
        ],
        out_specs=pl.BlockSpec((tm, tn), lambda j, i: (i, j)),
        out_shape=jax.ShapeDtypeStruct((t, n), BF16),
        scratch_shapes=[pltpu.VMEM((d, tn), BF16), pltpu.VMEM((d, tn), BF16)],
        compiler_params=_cparams("parallel", "arbitrary"),
        name="merge",
    )(ya, yb, wa, wb, proj, proj, gate_bias, gate_bias)


def _matmul_resid_kernel(a_ref, w_ref, r_ref, o_ref, ws_ref):
    @pl.when(pl.program_id(1) == 0)
    def _():
        _cast_weight(w_ref, ws_ref)

    o_ref[...] = r_ref[...] + _dot(a_ref[...], ws_ref[...])


def _matmul_resid(a, w, layer, resid, *, tm, tn, name):
    t, k = a.shape
    n = w.shape[2]
    return pl.pallas_call(
        _matmul_resid_kernel,
        grid=(n // tn, t // tm),
        in_specs=[
            pl.BlockSpec((tm, k), lambda j, i: (i, 0)),
            pl.BlockSpec((1, k, tn), lambda j, i: (layer, 0, j)),
            pl.BlockSpec((tm, tn), lambda j, i: (i, j)),
        ],
        out_specs=pl.BlockSpec((tm, tn), lambda j, i: (i, j)),
        out_shape=jax.ShapeDtypeStruct((t, n), F32),
        scratch_shapes=[pltpu.VMEM((k, tn), BF16)],
        compiler_params=_cparams("parallel", "arbitrary"),
        name=name,
    )(a, w, resid)


def _ffn_gu_kernel(x_ref, g_ref, wg_ref, wu_ref, o_ref, h_ref):
    @pl.when(pl.program_id(1) == 0)
    def _():
        h_ref[...] = _rmsnorm_rows(x_ref[...], g_ref[...]).astype(BF16)

    h = h_ref[...]
    gate = _dot(h, wg_ref[0].astype(BF16))
    up = _dot(h, wu_ref[0].astype(BF16))
    o_ref[...] = (_silu(gate) * up).astype(o_ref.dtype)


def _ffn_gu(x, g, w_gu, layer, *, tm=1024, tf=512):
    t, d = x.shape
    f = w_gu.shape[2] // 2
    nf = f // tf
    return pl.pallas_call(
        _ffn_gu_kernel,
        grid=(t // tm, nf),
        in_specs=[
            pl.BlockSpec((tm, d), lambda i, j: (i, 0)),
            pl.BlockSpec((1, d), lambda i, j: (0, 0)),
            pl.BlockSpec((1, d, tf), lambda i, j: (layer, 0, j)),
            pl.BlockSpec((1, d, tf), lambda i, j: (layer, 0, nf + j)),
        ],
        out_specs=pl.BlockSpec((tm, tf), lambda i, j: (i, j)),
        out_shape=jax.ShapeDtypeStruct((t, f), BF16),
        scratch_shapes=[pltpu.VMEM((tm, d), BF16)],
        compiler_params=_cparams("parallel", "arbitrary"),
        name="ffn_gu",
    )(x, g, w_gu, w_gu)


def _pack_pairs(lo, hi):
    lo_bits = lax.bitcast_convert_type(lo.astype(BF16).astype(F32), jnp.uint32)
    hi_bits = lax.bitcast_convert_type(hi.astype(BF16).astype(F32), jnp.uint32)
    return (hi_bits & jnp.uint32(0xFFFF0000)) | (lo_bits >> 16)


def _unpack_pairs(packed):
    lo = lax.bitcast_convert_type(packed << 16, F32)
    hi = lax.bitcast_convert_type(packed & jnp.uint32(0xFFFF0000), F32)
    return lo, hi


def _router_kernel(x_ref, g_ref, wr_ref, br_ref, h_ref, r_ref):
    h = _rmsnorm_rows(x_ref[...], g_ref[...])
    half = h.shape[1] // 2
    h_ref[...] = _pack_pairs(h[:, :half], h[:, half:])
    h1, h2, h3 = _split3(h)
    w = wr_ref[...]
    w1, w2, w3 = _split3(w)
    logits = (_dot(h1, w1) + (_dot(h1, w2) + _dot(h2, w1))
              + (_dot(h1, w3) + _dot(h2, w2) + _dot(h3, w1))) + br_ref[...]
    lane = lax.broadcasted_iota(jnp.int32, logits.shape, 1)
    logits = jnp.where(lane < N_EXPERTS, logits, -jnp.inf)
    m1 = jnp.max(logits, axis=-1, keepdims=True)
    i1 = jnp.min(jnp.where(logits == m1, lane, SM_W), axis=-1, keepdims=True)
    rest = jnp.where(lane == i1, -jnp.inf, logits)
    m2 = jnp.max(rest, axis=-1, keepdims=True)
    i2 = jnp.min(jnp.where(rest == m2, lane, SM_W), axis=-1, keepdims=True)
    e = jnp.exp(m2 - m1)
    p1 = 1.0 / (1.0 + e)
    p2 = e / (1.0 + e)
    r_ref[...] = jnp.where(lane == 0, i1.astype(F32),
                           jnp.where(lane == 1, i2.astype(F32),
                                     jnp.where(lane == 2, p1, jnp.where(lane == 3, p2, 0.0))))


def _router(x, g, w_router, b_router, *, tm=512):
    t, d = x.shape
    return pl.pallas_call(
        _router_kernel,
        grid=(t // tm,),
        in_specs=[
            pl.BlockSpec((tm, d), lambda i: (i, 0)),
            pl.BlockSpec((1, d), lambda i: (0, 0)),
            pl.BlockSpec((d, SM_W), lambda i: (0, 0)),
            pl.BlockSpec((1, SM_W), lambda i: (0, 0)),
        ],
        out_specs=[
            pl.BlockSpec((tm, d // 2), lambda i: (i, 0)),
            pl.BlockSpec((tm, SM_W), lambda i: (i, 0)),
        ],
        out_shape=[
            jax.ShapeDtypeStruct((t, d // 2), jnp.uint32),
            jax.ShapeDtypeStruct((t, SM_W), F32),
        ],
        compiler_params=_cparams("parallel"),
        name="router",
    )(x, g, w_router, b_router)


def _row_copy(src_ref, dst_ref, sem, src_row, dst_row):
    return pltpu.make_async_copy(src_ref.at[pl.ds(src_row, 1), :], dst_ref.at[pl.ds(dst_row, 1), :], sem)


def _gather_kernel(idx_ref, src_ref, o_ref, buf_ref, sem):
    rows = buf_ref.shape[0]
    base = pl.program_id(0) * rows

    def start(blk, carry):
        for u in range(DMA_UNROLL):
            r = blk * DMA_UNROLL + u
            _row_copy(src_ref, buf_ref, sem, idx_ref[base + r], r).start(priority=u % 2)
        return carry

    lax.fori_loop(0, rows // DMA_UNROLL, start, 0)

    def wait(blk, carry):
        for u in range(DMA_UNROLL):
            _row_copy(src_ref, buf_ref, sem, 0, blk * DMA_UNROLL + u).wait()
        return carry

    lax.fori_loop(0, rows // DMA_UNROLL, wait, 0)
    lo, hi = _unpack_pairs(buf_ref[...])
    half = buf_ref.shape[1]
    o_ref[:, :half] = lo.astype(o_ref.dtype)
    o_ref[:, half:] = hi.astype(o_ref.dtype)


def _gather_rows(src, idx, *, rows=512):
    p = idx.shape[0]
    half = src.shape[1]
    d = 2 * half
    return pl.pallas_call(
        _gather_kernel,
        grid_spec=pltpu.PrefetchScalarGridSpec(
            num_scalar_prefetch=1,
            grid=(p // rows,),
            in_specs=[pl.BlockSpec(memory_space=pl.ANY)],
            out_specs=pl.BlockSpec((rows, d), lambda i, idx_ref: (i, 0)),
            scratch_shapes=[pltpu.VMEM((rows, half), src.dtype), pltpu.SemaphoreType.DMA(())],
        ),
        out_shape=jax.ShapeDtypeStruct((p, d), BF16),
        compiler_params=_cparams("arbitrary"),
        name="moe_gather",
    )(idx, src)


def _expert_changed(te_ref):
    i = pl.program_id(1)
    return jnp.logical_or(i == 0, te_ref[i] != te_ref[jnp.maximum(i - 1, 0)])


def _moe_gu_kernel(te_ref, nu_ref, a_ref, wg_ref, wu_ref, o_ref, wgs_ref, wus_ref):
    used = pl.program_id(1) < nu_ref[0]

    @pl.when(_expert_changed(te_ref))
    def _():
        _cast_weight(wg_ref, wgs_ref)
        _cast_weight(wu_ref, wus_ref)

    @pl.when(used)
    def _():
        a = a_ref[...]
        o_ref[...] = (_silu(_dot(a, wgs_ref[...])) * _dot(a, wus_ref[...])).astype(o_ref.dtype)

    @pl.when(jnp.logical_not(used))
    def _():
        o_ref[...] = jnp.zeros_like(o_ref)


def _moe_gu(a, w_gu, layer, tile_expert, n_used, *, tm, tf=1024):
    p, d = a.shape
    f = w_gu.shape[3] // 2
    nf = f // tf
    return pl.pallas_call(
        _moe_gu_kernel,
        grid_spec=pltpu.PrefetchScalarGridSpec(
            num_scalar_prefetch=2,
            grid=(nf, p // tm),
            in_specs=[
                pl.BlockSpec((tm, d), lambda j, i, te, nu: (i, 0)),
                pl.BlockSpec((1, 1, d, tf), lambda j, i, te, nu: (layer, te[i], 0, j)),
                pl.BlockSpec((1, 1, d, tf), lambda j, i, te, nu: (layer, te[i], 0, nf + j)),
            ],
            out_specs=pl.BlockSpec((tm, tf), lambda j, i, te, nu: (i, j)),
            scratch_shapes=[pltpu.VMEM((d, tf), BF16), pltpu.VMEM((d, tf), BF16)],
        ),
        out_shape=jax.ShapeDtypeStruct((p, f), BF16),
        compiler_params=_cparams("parallel", "arbitrary"),
        name="moe_gu",
    )(tile_expert, n_used, a, w_gu, w_gu)


def _moe_down_kernel(te_ref, nu_ref, a_ref, w_ref, o_ref, ws_ref):
    used = pl.program_id(1) < nu_ref[0]

    @pl.when(_expert_changed(te_ref))
    def _():
        _cast_weight(w_ref, ws_ref)

    @pl.when(used)
    def _():
        y = _dot(a_ref[...], ws_ref[...])
        half = y.shape[1] // 2
        o_ref[...] = _pack_pairs(y[:, :half], y[:, half:])

    @pl.when(jnp.logical_not(used))
    def _():
        o_ref[...] = jnp.zeros_like(o_ref)


def _moe_down(a, w_down, layer, tile_expert, n_used, *, tm, tn=MOE_DOWN_TN):
    p, f = a.shape
    n = w_down.shape[3]
    return pl.pallas_call(
        _moe_down_kernel,
        grid_spec=pltpu.PrefetchScalarGridSpec(
            num_scalar_prefetch=2,
            grid=(n // tn, p // tm),
            in_specs=[
                pl.BlockSpec((tm, f), lambda j, i, te, nu: (i, 0)),
                pl.BlockSpec((1, 1, f, tn), lambda j, i, te, nu: (layer, te[i], 0, j)),
            ],
            out_specs=pl.BlockSpec((tm, tn // 2), lambda j, i, te, nu: (i, j)),
            scratch_shapes=[pltpu.VMEM((f, tn), BF16)],
        ),
        out_shape=jax.ShapeDtypeStruct((p, n // 2), jnp.uint32),
        compiler_params=_cparams("parallel", "arbitrary"),
        name="moe_down",
    )(tile_expert, n_used, a, w_down)


def _combine_kernel(p1_ref, p2_ref, y_ref, x_ref, r_ref, g_ref, o_ref, b1_ref, b2_ref, sem, *, final_norm):
    rows = b1_ref.shape[0]
    base = pl.program_id(0) * rows

    def start(blk, carry):
        for u in range(DMA_UNROLL):
            r = blk * DMA_UNROLL + u
            _row_copy(y_ref, b1_ref, sem.at[0], p1_ref[base + r], r).start(priority=0)
            _row_copy(y_ref, b2_ref, sem.at[1], p2_ref[base + r], r).start(priority=1)
        return carry

    lax.fori_loop(0, rows // DMA_UNROLL, start, 0)

    def wait(blk, carry):
        for u in range(DMA_UNROLL):
            r = blk * DMA_UNROLL + u
            _row_copy(y_ref, b1_ref, sem.at[0], 0, r).wait()
            _row_copy(y_ref, b2_ref, sem.at[1], 0, r).wait()
        return carry

    lax.fori_loop(0, rows // DMA_UNROLL, wait, 0)
    route = r_ref[...]
    w1 = route[:, 2:3]
    w2 = route[:, 3:4]
    lo1, hi1 = _unpack_pairs(b1_ref[...])
    lo2, hi2 = _unpack_pairs(b2_ref[...])
    lo = w1 * lo1 + w2 * lo2
    hi = w1 * hi1 + w2 * hi2
    hw = MOE_DOWN_TN // 2
    pieces = []
    for j in range(lo.shape[1] // hw):
        pieces += [lo[:, j * hw:(j + 1) * hw], hi[:, j * hw:(j + 1) * hw]]
    x_new = x_ref[...] + jnp.concatenate(pieces, axis=1)
    o_ref[...] = _rmsnorm_rows(x_new, g_ref[...]) if final_norm else x_new


def _combine(y_sorted, x, route, pos1, pos2, norm_g, *, final_norm, rows=512):
    t, d = x.shape
    half = y_sorted.shape[1]
    return pl.pallas_call(
        functools.partial(_combine_kernel, final_norm=final_norm),
        grid_spec=pltpu.PrefetchScalarGridSpec(
            num_scalar_prefetch=2,
            grid=(t // rows,),
            in_specs=[
                pl.BlockSpec(memory_space=pl.ANY),
                pl.BlockSpec((rows, d), lambda i, p1, p2: (i, 0)),
                pl.BlockSpec((rows, SM_W), lambda i, p1, p2: (i, 0)),
                pl.BlockSpec((1, d), lambda i, p1, p2: (0, 0)),
            ],
            out_specs=pl.BlockSpec((rows, d), lambda i, p1, p2: (i, 0)),
            scratch_shapes=[pltpu.VMEM((rows, half), jnp.uint32), pltpu.VMEM((rows, half), jnp.uint32),
                            pltpu.SemaphoreType.DMA((2,))],
        ),
        out_shape=jax.ShapeDtypeStruct((t, d), F32),
        compiler_params=_cparams("arbitrary"),
        name="moe_combine",
    )(pos1, pos2, y_sorted, x, route, norm_g)


MOE_TM = 512


def _moe_plan(route, tm):
    t = route.shape[0]
    n_tiles = (t * TOP_K) // tm + N_EXPERTS
    eid = route[:, :TOP_K].astype(jnp.int32).reshape(-1)
    onehot = (eid[:, None] == jnp.arange(N_EXPERTS, dtype=jnp.int32)[None, :]).astype(jnp.int32)
    csum = jnp.cumsum(onehot, axis=0)
    rank = jnp.sum((csum - onehot) * onehot, axis=1)
    counts = csum[-1]
    tiles_per = (counts + tm - 1) // tm
    tile_end = jnp.cumsum(tiles_per)
    start = (tile_end - tiles_per) * tm
    pos = (jnp.sum(onehot * start[None, :], axis=1) + rank).astype(jnp.int32)
    n_used = tile_end[-1].astype(jnp.int32)
    tile_id = jnp.arange(n_tiles, dtype=jnp.int32)
    tile_expert = jnp.sum((tile_id[:, None] >= tile_end[None, :]).astype(jnp.int32), axis=1)
    tile_expert = jnp.minimum(tile_expert, N_EXPERTS - 1).astype(jnp.int32)
    last_expert = jnp.max(jnp.where(counts > 0, jnp.arange(N_EXPERTS, dtype=jnp.int32), 0))
    tile_expert = jnp.where(tile_id < n_used, tile_expert, last_expert).astype(jnp.int32)
    token = jnp.repeat(jnp.arange(t, dtype=jnp.int32), TOP_K)
    row_token = jnp.zeros((n_tiles * tm,), jnp.int32).at[pos].set(token)
    pos = pos.reshape(t, TOP_K)
    return row_token, tile_expert, n_used.reshape(1), pos[:, 0], pos[:, 1]


def _moe_ffn(x, g, w_router, b_router, w_gu, w_down, layer, final_g, final_norm):
    h, route = _router(x, g, w_router, b_router)
    row_token, tile_expert, n_used, pos1, pos2 = _moe_plan(route, MOE_TM)
    a = _gather_rows(h, row_token)
    act = _moe_gu(a, w_gu, layer, tile_expert, n_used, tm=MOE_TM)
    y = _moe_down(act, w_down, layer, tile_expert, n_used, tm=MOE_TM)
    return _combine(y, x, route, pos1, pos2, final_g, final_norm=final_norm)


def _final_norm_kernel(x_ref, g_ref, o_ref):
    o_ref[...] = _rmsnorm_rows(x_ref[...], g_ref[...])


def _final_norm(x, g, *, tm=512):
    t, d = x.shape
    return pl.pallas_call(
        _final_norm_kernel,
        grid=(t // tm,),
        in_specs=[pl.BlockSpec((tm, d), lambda i: (i, 0)), pl.BlockSpec((1, d), lambda i: (0, 0))],
        out_specs=pl.BlockSpec((tm, d), lambda i: (i, 0)),
        out_shape=jax.ShapeDtypeStruct((t, d), F32),
        compiler_params=_cparams("parallel"),
        name="final_norm",
    )(x, g)


def _lane_row(vec, offset):
    return jnp.zeros((1, SM_W), F32).at[0, offset:offset + vec.shape[0]].set(vec.astype(F32))


def _mixer(x, l, bsz, seq, p, consts):
    tril, triu, expand = consts
    h, small, small_t = _norm_small(x, p["norm_mix_g"][l][None, :], p["w_in_t"], l)
    conv_w = jnp.zeros((CONV_K, PC_TOTAL), F32)
    conv_w = conv_w.at[:, PC_Q:PC_V].set(p["qk_conv_w"][l]).at[:, PC_XBC:PC_GATE].set(p["ssd_conv_w"][l])
    conv_b = jnp.zeros((1, PC_TOTAL), F32)
    conv_b = conv_b.at[0, PC_Q:PC_V].set(p["qk_conv_b"][l]).at[0, PC_XBC:PC_GATE].set(p["ssd_conv_b"][l])
    proj = _proj(h, p["w_in_t"], l, conv_w, conv_b, seq=seq)

    if_bias = p["if_bias"][l]
    y_a = _mlstm(proj, small, small_t,
                 _lane_row(if_bias, 0),
                 jnp.broadcast_to(if_bias.astype(F32)[:, None], (2 * ML_HEADS, SEQ_CHUNK)),
                 p["mh_norm_g"][l][None, :], tril, triu, bsz=bsz, seq=seq)
    y_b = _ssd(proj, small, small_t,
               _lane_row(p["dt_bias"][l], SM_DT),
               jnp.broadcast_to(p["dt_bias"][l].astype(F32)[:, None], (SSD_HEADS, SEQ_CHUNK)),
               _lane_row(p["a_log"][l], SM_DT),
               jnp.broadcast_to(p["a_log"][l].astype(F32)[:, None], (SSD_HEADS, SEQ_CHUNK)),
               jnp.repeat(p["d_skip"][l].astype(F32), SSD_P)[None, :],
               p["ssd_norm_g"][l][None, :], tril, triu, expand, bsz=bsz, seq=seq)
    merged = _merge(y_a, y_b, p["w_branch_a"], p["w_branch_b"], l, proj, p["gate_bias"][l][None, :])
    return _matmul_resid(merged, p["w_out"], l, x, tm=1024, tn=1024, name="out_proj")


def kernel(x, norm_mix_g, w_in, if_bias, qk_conv_w, qk_conv_b, mh_norm_g, ssd_conv_w, ssd_conv_b,
           dt_bias, a_log, d_skip, ssd_norm_g, gate_bias, w_branch_a, w_branch_b, w_out, norm_ffn_g,
           ffn_w_gu, ffn_w_down, router_w, router_b, exp_w_gu, exp_w_down, norm_final_g):
    p = dict(norm_mix_g=norm_mix_g, w_in_t=jnp.swapaxes(w_in, 1, 2), if_bias=if_bias, qk_conv_w=qk_conv_w,
             qk_conv_b=qk_conv_b, mh_norm_g=mh_norm_g, ssd_conv_w=ssd_conv_w, ssd_conv_b=ssd_conv_b,
             dt_bias=dt_bias, a_log=a_log, d_skip=d_skip, ssd_norm_g=ssd_norm_g, gate_bias=gate_bias,
             w_branch_a=w_branch_a, w_branch_b=w_branch_b, w_out=w_out)
    bsz, seq, d = x.shape
    assert d == D_MODEL and seq % SEQ_CHUNK == 0 and (bsz * seq) % 2048 == 0, x.shape
    x = x.reshape(bsz * seq, d)

    idx = jnp.arange(SEQ_CHUNK, dtype=jnp.int32)
    tril = (idx[:, None] >= idx[None, :]).astype(BF16)
    triu = (idx[:, None] <= idx[None, :]).astype(BF16)
    lane = jnp.arange(SM_W, dtype=jnp.int32)[:, None]
    chan = jnp.arange(SSD_WIDTH, dtype=jnp.int32)[None, :]
    expand = (lane == SM_DT + chan // SSD_P).astype(BF16)
    consts = (tril, triu, expand)

    for l in range(DEPTH):
        x = _mixer(x, l, bsz, seq, p, consts)
        g = norm_ffn_g[l][None, :]
        if l % 2 == 0:
            act = _ffn_gu(x, g, ffn_w_gu, l // 2)
            x = _matmul_resid(act, ffn_w_down, l // 2, x, tm=512, tn=512, name="ffn_down")
        else:
            w_r = jnp.pad(router_w[l // 2], ((0, 0), (0, SM_W - N_EXPERTS)))
            b_r = _lane_row(router_b[l // 2], 0)
            x = _moe_ffn(x, g, w_r, b_r, exp_w_gu, exp_w_down, l // 2, norm_final_g[None, :], l == DEPTH - 1)
    if DEPTH % 2 == 1:
        x = _final_norm(x, norm_final_g[None, :])
    return x.reshape(bsz, seq, d)
```

```python
import functools
import math

import jax
import jax.numpy as jnp
from jax import lax
from jax.experimental import pallas as pl
from jax.experimental.pallas import tpu as pltpu

F32 = jnp.float32
BF16 = jnp.bfloat16

D_MODEL = 2048
DEPTH = 4
EPS = 1e-6
CONV_K = 4
ML_HEADS = 4
ML_DV = 512
ML_DK = 256
ML_QK = ML_HEADS * ML_DK
ML_WIDTH = ML_HEADS * ML_DV
SSD_HEADS = 32
SSD_P = 64
SSD_GROUPS = 8
SSD_N = 128
SSD_R = SSD_HEADS // SSD_GROUPS
SSD_WIDTH = SSD_HEADS * SSD_P
SSD_GN = SSD_GROUPS * SSD_N
SSD_CONV = SSD_WIDTH + 2 * SSD_GN
SSD_GW = SSD_R * SSD_P
FFN_DENSE = 5632
N_EXPERTS = 8
TOP_K = 2
FFN_EXPERT = 4096

PC_Q = 0
PC_K = 1024
PC_V = 2048
PC_O = 4096
PC_Z = 6144
PC_XBC = 8192
PC_GATE = 12288
PC_TOTAL = 16384
SRC_IF = 6144
SRC_DT = 12296
SHIFT_ZXBC = 8
SHIFT_GATE = 40
PROJ_TN = 1024
SM_W = 128
SM_COLS = 2 * SM_W
SM_DT = SRC_DT - (SRC_DT // SM_W) * SM_W

V7X_VMEM_BYTES = 64 * 1024 * 1024
VMEM_LIMIT = 58 * 1024 * 1024

DMA_UNROLL = 16
MOE_DOWN_TN = 1024
SEQ_CHUNK = 256
CARRY = 8


def _cparams(*sem):
    return pltpu.CompilerParams(dimension_semantics=sem, vmem_limit_bytes=VMEM_LIMIT)


def _sigmoid(x):
    return 1.0 / (1.0 + jnp.exp(-x))


def _silu(x):
    half = 0.5 * x
    return half * jnp.tanh(half) + half


def _softplus(x):
    return jnp.maximum(x, 0.0) + jnp.log(1.0 + jnp.exp(-jnp.abs(x)))


def _log_sigmoid(x):
    return jnp.minimum(x, 0.0) - jnp.log(1.0 + jnp.exp(-jnp.abs(x)))


def _split3(x):
    x1 = x.astype(BF16)
    r1 = x - x1.astype(F32)
    x2 = r1.astype(BF16)
    x3 = (r1 - x2.astype(F32)).astype(BF16)
    return x1, x2, x3


def _dot(a, b):
    return jnp.dot(a, b, preferred_element_type=F32)


def _dot_nt(a, b):
    return lax.dot_general(a, b, (((1,), (1,)), ((), ())), preferred_element_type=F32)


def _dot_tn(a, b):
    return lax.dot_general(a, b, (((0,), (0,)), ((), ())), preferred_element_type=F32)


def _dot_f32_left(x, m01):
    x1, x2, x3 = _split3(x)
    return _dot(x1, m01) + _dot(x2, m01) + _dot(x3, m01)


def _dot_f32_right(m01, x):
    x1, x2, x3 = _split3(x)
    return _dot(m01, x1) + _dot(m01, x2) + _dot(m01, x3)


def _rmsnorm_rows(x, g):
    ms = jnp.mean(x * x, axis=-1, keepdims=True)
    return (x * lax.rsqrt(ms + EPS)) * g


def _norm_small_kernel(x_ref, g_ref, w0_ref, w1_ref, h_ref, s_ref, st_ref):
    hb = _rmsnorm_rows(x_ref[...], g_ref[...]).astype(BF16)
    h_ref[...] = hb
    wt = jnp.concatenate([w0_ref[0], w1_ref[0]], axis=0).astype(BF16)
    s_ref[...] = _dot_nt(hb, wt)
    st_ref[...] = _dot_nt(wt, hb)


def _norm_small(x, g, w_in_t, layer, *, tm=1024):
    t, d = x.shape
    return pl.pallas_call(
        _norm_small_kernel,
        grid=(t // tm,),
        in_specs=[
            pl.BlockSpec((tm, d), lambda i: (i, 0)),
            pl.BlockSpec((1, d), lambda i: (0, 0)),
            pl.BlockSpec((1, SM_W, d), lambda i: (layer, SRC_IF // SM_W, 0)),
            pl.BlockSpec((1, SM_W, d), lambda i: (layer, SRC_DT // SM_W, 0)),
        ],
        out_specs=[
            pl.BlockSpec((tm, d), lambda i: (i, 0)),
            pl.BlockSpec((tm, SM_COLS), lambda i: (i, 0)),
            pl.BlockSpec((SM_COLS, tm), lambda i: (0, i)),
        ],
        out_shape=[
            jax.ShapeDtypeStruct((t, d), BF16),
            jax.ShapeDtypeStruct((t, SM_COLS), F32),
            jax.ShapeDtypeStruct((SM_COLS, t), F32),
        ],
        compiler_params=_cparams("parallel"),
        name="norm_small",
    )(x, g, w_in_t, w_in_t)


CAST_ROWS = 256


def _conv_silu(acc, carry, cw_ref, cb_ref):
    cat = jnp.concatenate([carry, acc], axis=0)
    y = cb_ref[...] + cw_ref[CONV_K - 1:CONV_K, :] * acc
    for back in range(1, CONV_K):
        tap = pltpu.roll(cat, back, axis=0)[CARRY:, :]
        y = y + cw_ref[CONV_K - 1 - back:CONV_K - back, :] * tap
    return _silu(y)


PROJ_SUB_ROWS = 256


def _proj_kernel(h_ref, wt_ref, cw_ref, cb_ref, o_ref, ws_ref, carry_ref, acc_ref, *, tiles_per_seq):
    j = pl.program_id(0)
    i = pl.program_id(1)
    tn = ws_ref.shape[1]

    @pl.when(i == 0)
    def _():
        for r in range(0, tn, CAST_ROWS):
            ws_ref[:, r:r + CAST_ROWS] = wt_ref[0, r:r + CAST_ROWS, :].T.astype(BF16)

    @pl.when(i % tiles_per_seq == 0)
    def _():
        carry_ref[...] = jnp.zeros_like(carry_ref)

    is_conv = jnp.logical_or(j < PC_V // tn, jnp.logical_and(j >= PC_XBC // tn, j < PC_GATE // tn))
    is_sigmoid = jnp.logical_and(j >= PC_O // tn, j < PC_Z // tn)
    is_silu = jnp.logical_and(j >= PC_Z // tn, j < PC_XBC // tn)
    is_plain = jnp.logical_not(jnp.logical_or(is_conv, jnp.logical_or(is_sigmoid, is_silu)))

    sub = PROJ_SUB_ROWS
    row_blocks = [slice(r, r + sub) for r in range(0, h_ref.shape[0], sub)]

    def sub_dot(s):
        acc_ref[s % 2] = _dot(h_ref[row_blocks[s], :], ws_ref[...])

    def pipelined(epilogue):
        sub_dot(0)
        for s, rows in enumerate(row_blocks):
            if s + 1 < len(row_blocks):
                sub_dot(s + 1)
            o_ref[rows, :] = epilogue(acc_ref[s % 2]).astype(o_ref.dtype)

    @pl.when(is_conv)
    def _():
        carry = [carry_ref[...]]

        def conv(acc):
            y = _conv_silu(acc, carry[0], cw_ref, cb_ref)
            carry[0] = acc[sub - CARRY:, :]
            return y

        pipelined(conv)
        carry_ref[...] = carry[0]

    @pl.when(is_sigmoid)
    def _():
        pipelined(_sigmoid)

    @pl.when(is_silu)
    def _():
        pipelined(_silu)

    @pl.when(is_plain)
    def _():
        for rows in row_blocks:
            o_ref[rows, :] = _dot(h_ref[rows, :], ws_ref[...]).astype(o_ref.dtype)


def _proj_src_row(j, tn):
    past_z = (j >= PC_Z // tn).astype(jnp.int32)
    past_gate = (j >= PC_GATE // tn).astype(jnp.int32)
    shift8 = past_z * (SHIFT_ZXBC // 8) + past_gate * ((SHIFT_GATE - SHIFT_ZXBC) // 8)
    return pl.multiple_of(j * tn + 8 * shift8, 8)


def _proj(h, w_in_t, layer, conv_w, conv_b, *, seq, tm=2048):
    t, d = h.shape
    tn = PROJ_TN
    assert t % tm == 0 and seq % tm == 0, (t, seq, tm)
    return pl.pallas_call(
        functools.partial(_proj_kernel, tiles_per_seq=seq // tm),
        grid=(PC_TOTAL // tn, t // tm),
        in_specs=[
            pl.BlockSpec((tm, d), lambda j, i: (i, 0)),
            pl.BlockSpec((pl.Element(1), pl.Element(tn), pl.Element(d)),
                         lambda j, i: (layer, _proj_src_row(j, tn), 0)),
            pl.BlockSpec((CONV_K, tn), lambda j, i: (0, j)),
            pl.BlockSpec((1, tn), lambda j, i: (0, j)),
        ],
        out_specs=pl.BlockSpec((tm, tn), lambda j, i: (i, j)),
        out_shape=jax.ShapeDtypeStruct((t, PC_TOTAL), BF16),
        scratch_shapes=[pltpu.VMEM((d, tn), BF16), pltpu.VMEM((CARRY, tn), F32),
                        pltpu.VMEM((2, PROJ_SUB_ROWS, tn), F32)],
        compiler_params=_cparams("parallel", "arbitrary"),
        name="proj",
    )(h, w_in_t, conv_w, conv_b)


def _mlstm_kernel(q_ref, k_ref, v_ref, o_ref, sm_ref, smt_ref, ifr_ref, ifc_ref,
                  g_ref, tril_ref, triu_ref, y_ref, c_ref, n_ref, m_ref):
    length = q_ref.shape[0]

    @pl.when(pl.program_id(1) == 0)
    def _():
        c_ref[...] = jnp.zeros_like(c_ref)
        n_ref[...] = jnp.zeros_like(n_ref)
        m_ref[...] = jnp.zeros_like(m_ref)

    cols = sm_ref[...] + ifr_ref[...]
    rows = smt_ref[...] + ifc_ref[...]
    b_cols = _dot_f32_right(tril_ref[...], _log_sigmoid(cols))
    b_rows = _dot_f32_left(_log_sigmoid(rows), triu_ref[...])
    r_i = lax.broadcasted_iota(jnp.int32, (length, length), 0)
    c_i = lax.broadcasted_iota(jnp.int32, (length, length), 1)
    causal = r_i >= c_i

    for h in range(ML_HEADS):
        kb = k_ref[:, h * ML_DK:(h + 1) * ML_DK]
        q_h = q_ref[:, h * ML_DK:(h + 1) * ML_DK].astype(F32) * (ML_DK ** -0.5)
        k_h = kb.astype(F32)
        qb = q_h.astype(BF16)
        v_h = v_ref[:, h * ML_DV:(h + 1) * ML_DV]
        i_col = cols[:, h:h + 1]
        b_col = b_cols[:, ML_HEADS + h:ML_HEADS + h + 1]
        i_row = rows[h:h + 1, :]
        b_row = b_rows[ML_HEADS + h:ML_HEADS + h + 1, :]
        m_prev = m_ref[h:h + 1, 0:1]
        c_prev = c_ref[h]
        n_prev = n_ref[h:h + 1, :]

        log_d = jnp.where(causal, b_col - b_row + i_row, -jnp.inf)
        m_inter = b_col + m_prev
        m_t = jnp.maximum(m_inter, jnp.max(log_d, axis=-1, keepdims=True))
        scores = _dot_nt(qb, kb) * jnp.exp(log_d - m_t)
        inter = jnp.exp(m_inter - m_t)
        num = _dot(scores.astype(BF16), v_h) + inter * _dot(qb, c_prev.astype(BF16))
        den = (jnp.sum(scores, axis=-1, keepdims=True)
               + inter * jnp.sum(q_h * n_prev, axis=-1, keepdims=True))
        hh = num / jnp.maximum(jnp.abs(den), jnp.exp(-m_t))
        hh = hh * lax.rsqrt(jnp.mean(hh * hh, axis=-1, keepdims=True) + EPS)
        gate = o_ref[:, h * ML_DV:(h + 1) * ML_DV].astype(F32)
        y_ref[:, h * ML_DV:(h + 1) * ML_DV] = (
            hh * g_ref[:, h * ML_DV:(h + 1) * ML_DV] * gate).astype(y_ref.dtype)

        b_last = b_col[length - 1:length, :]
        log_w_row = b_last - b_row + i_row
        m_new = jnp.maximum(b_last + m_prev, jnp.max(log_w_row, axis=-1, keepdims=True))
        w_col = jnp.exp(b_last - b_col + i_col - m_new)
        decay = jnp.exp(b_last + m_prev - m_new)
        vw = (v_h.astype(F32) * w_col).astype(BF16)
        c_ref[h] = decay * c_prev + _dot_tn(kb, vw)
        n_ref[h:h + 1, :] = decay * n_prev + jnp.sum(k_h * w_col, axis=0, keepdims=True)
        m_ref[h:h + 1, :] = jnp.broadcast_to(m_new, (1, m_ref.shape[1]))


def _mlstm(proj, small, small_t, if_row, if_col, norm_g, tril, triu, *, bsz, seq):
    length = SEQ_CHUNK
    nc = seq // length
    t = bsz * seq
    row = lambda b, c: b * nc + c
    return pl.pallas_call(
        _mlstm_kernel,
        grid=(bsz, nc),
        in_specs=[
            pl.BlockSpec((length, ML_QK), lambda b, c: (row(b, c), PC_Q // ML_QK)),
            pl.BlockSpec((length, ML_QK), lambda b, c: (row(b, c), PC_K // ML_QK)),
            pl.BlockSpec((length, ML_WIDTH), lambda b, c: (row(b, c), PC_V // ML_WIDTH)),
            pl.BlockSpec((length, ML_WIDTH), lambda b, c: (row(b, c), PC_O // ML_WIDTH)),
            pl.BlockSpec((length, SM_W), lambda b, c: (row(b, c), 0)),
            pl.BlockSpec((2 * ML_HEADS, length), lambda b, c: (0, row(b, c))),
            pl.BlockSpec((1, SM_W), lambda b, c: (0, 0)),
            pl.BlockSpec((2 * ML_HEADS, length), lambda b, c: (0, 0)),
            pl.BlockSpec((1, ML_WIDTH), lambda b, c: (0, 0)),
            pl.BlockSpec((length, length), lambda b, c: (0, 0)),
            pl.BlockSpec((length, length), lambda b, c: (0, 0)),
        ],
        out_specs=pl.BlockSpec((length, ML_WIDTH), lambda b, c: (row(b, c), 0)),
        out_shape=jax.ShapeDtypeStruct((t, ML_WIDTH), BF16),
        scratch_shapes=[
            pltpu.VMEM((ML_HEADS, ML_DK, ML_DV), F32),
            pltpu.VMEM((2 * ML_HEADS, ML_DK), F32),
            pltpu.VMEM((2 * ML_HEADS, 128), F32),
        ],
        compiler_params=_cparams("parallel", "arbitrary"),
        name="mlstm",
    )(proj, proj, proj, proj, small, small_t, if_row, if_col, norm_g, tril, triu)


def _ssd_kernel(xbc_ref, z_ref, sm_ref, smt_ref, dtb_r_ref, dtb_c_ref, al_r_ref,
                al_c_ref, dsk_ref, g_ref, tril_ref, triu_ref, e_ref, y_ref, st_ref):
    length = xbc_ref.shape[0]

    @pl.when(pl.program_id(1) == 0)
    def _():
        st_ref[...] = jnp.zeros_like(st_ref)

    lane = lax.broadcasted_iota(jnp.int32, (1, SM_W), 1)
    head_lane = jnp.logical_and(lane >= SM_DT, lane < SM_DT + SSD_HEADS)
    dt_c = jnp.where(head_lane, _softplus(sm_ref[...] + dtb_r_ref[...]), 0.0)
    dt_r = _softplus(smt_ref[SM_DT:SM_DT + SSD_HEADS, :] + dtb_c_ref[...])
    a_c = dt_c * (-jnp.exp(al_r_ref[...]))
    a_r = dt_r * (-jnp.exp(al_c_ref[...]))
    acum_c = _dot_f32_right(tril_ref[...], a_c)
    acum_r = _dot_f32_left(a_r, triu_ref[...])
    a_last = acum_c[length - 1:length, :]

    e01 = e_ref[...]

    def expand(x):
        x1 = x.astype(BF16)
        x2 = (x - x1.astype(F32)).astype(BF16)
        return _dot(x1, e01) + _dot(x2, e01)

    dt_e = expand(dt_c)
    from_start_e = expand(jnp.exp(acum_c))
    to_end_e = expand(jnp.exp(a_last - acum_c))
    chunk_e = expand(jnp.broadcast_to(jnp.exp(a_last), (8, SM_W)))[0:1, :]

    r_i = lax.broadcasted_iota(jnp.int32, (length, length), 0)
    c_i = lax.broadcasted_iota(jnp.int32, (length, length), 1)
    causal = r_i >= c_i

    for g in range(SSD_GROUPS):
        ch = slice(g * SSD_GW, (g + 1) * SSD_GW)
        xs_g = xbc_ref[:, ch].astype(F32)
        bm = xbc_ref[:, SSD_WIDTH + g * SSD_N:SSD_WIDTH + (g + 1) * SSD_N]
        cm = xbc_ref[:, SSD_WIDTH + SSD_GN + g * SSD_N:SSD_WIDTH + SSD_GN + (g + 1) * SSD_N]
        xdt = xs_g * dt_e[:, ch]
        cb = _dot_nt(cm, bm)
        prev = st_ref[g]
        y = _dot(cm, prev.astype(BF16)) * from_start_e[:, ch] + dsk_ref[:, ch] * xs_g
        diag = []
        for r in range(SSD_R):
            h = g * SSD_R + r
            seg = acum_c[:, SM_DT + h:SM_DT + h + 1] - acum_r[h:h + 1, :]
            att = cb * jnp.exp(jnp.where(causal, seg, -jnp.inf))
            diag.append(_dot(att.astype(BF16), xdt[:, r * SSD_P:(r + 1) * SSD_P].astype(BF16)))
        y = y + jnp.concatenate(diag, axis=-1)
        st_ref[g] = prev * chunk_e[:, ch] + _dot_tn(bm, (xdt * to_end_e[:, ch]).astype(BF16))
        y = y * z_ref[:, ch].astype(F32)
        y = y * lax.rsqrt(jnp.mean(y * y, axis=-1, keepdims=True) + EPS)
        y_ref[:, ch] = (y * g_ref[:, ch]).astype(y_ref.dtype)


def _ssd(proj, small, small_t, dtb_row, dtb_col, al_row, al_col, dskip_e, norm_g,
         tril, triu, expand, *, bsz, seq):
    length = SEQ_CHUNK
    nc = seq // length
    t = bsz * seq
    row = lambda b, c: b * nc + c
    const = lambda b, c: (0, 0)
    return pl.pallas_call(
        _ssd_kernel,
        grid=(bsz, nc),
        in_specs=[
            pl.BlockSpec((length, SSD_CONV), lambda b, c: (row(b, c), PC_XBC // SSD_CONV)),
            pl.BlockSpec((length, SSD_WIDTH), lambda b, c: (row(b, c), PC_Z // SSD_WIDTH)),
            pl.BlockSpec((length, SM_W), lambda b, c: (row(b, c), 1)),
            pl.BlockSpec((2 * SSD_HEADS, length), lambda b, c: (SM_W // (2 * SSD_HEADS), row(b, c))),
            pl.BlockSpec((1, SM_W), const),
            pl.BlockSpec((SSD_HEADS, length), const),
            pl.BlockSpec((1, SM_W), const),
            pl.BlockSpec((SSD_HEADS, length), const),
            pl.BlockSpec((1, SSD_WIDTH), const),
            pl.BlockSpec((1, SSD_WIDTH), const),
            pl.BlockSpec((length, length), const),
            pl.BlockSpec((length, length), const),
            pl.BlockSpec((SM_W, SSD_WIDTH), const),
        ],
        out_specs=pl.BlockSpec((length, SSD_WIDTH), lambda b, c: (row(b, c), 0)),
        out_shape=jax.ShapeDtypeStruct((t, SSD_WIDTH), BF16),
        scratch_shapes=[pltpu.VMEM((SSD_GROUPS, SSD_N, SSD_GW), F32)],
        compiler_params=_cparams("parallel", "arbitrary"),
        name="ssd",
    )(proj, proj, small, small_t, dtb_row, dtb_col, al_row, al_col, dskip_e, norm_g, tril, triu, expand)


def _cast_weight(w_ref, ws_ref):
    lead = (0,) * (len(w_ref.shape) - 2)
    for r in range(0, ws_ref.shape[0], CAST_ROWS):
        ws_ref[r:r + CAST_ROWS, :] = w_ref[lead + (slice(r, r + CAST_ROWS), slice(None))].astype(BF16)


def _merge_kernel(ya_ref, yb_ref, wa_ref, wb_ref, ga_ref, gb_ref, ba_ref, bb_ref, o_ref, was_ref, wbs_ref):
    @pl.when(pl.program_id(1) == 0)
    def _():
        _cast_weight(wa_ref, was_ref)
        _cast_weight(wb_ref, wbs_ref)

    ga = _sigmoid(ga_ref[...].astype(F32) + ba_ref[...])
    gb = _sigmoid(gb_ref[...].astype(F32) + bb_ref[...])
    o_ref[...] = (ga * _dot(ya_ref[...], was_ref[...])
                  + gb * _dot(yb_ref[...], wbs_ref[...])).astype(o_ref.dtype)


def _merge(ya, yb, wa, wb, layer, proj, gate_bias, *, tm=1024, tn=512):
    t, d = ya.shape
    n = wa.shape[2]
    ga0 = PC_GATE // tn
    gb0 = (PC_GATE + n) // tn
    return pl.pallas_call(
        _merge_kernel,
        grid=(n // tn, t // tm),
        in_specs=[
            pl.BlockSpec((tm, d), lambda j, i: (i, 0)),
            pl.BlockSpec((tm, d), lambda j, i: (i, 0)),
            pl.BlockSpec((1, d, tn), lambda j, i: (layer, 0, j)),
            pl.BlockSpec((1, d, tn), lambda j, i: (layer, 0, j)),
            pl.BlockSpec((tm, tn), lambda j, i: (i, ga0 + j)),
            pl.BlockSpec((tm, tn), lambda j, i: (i, gb0 + j)),
            pl.BlockSpec((1, tn), lambda j, i: (0, j)),
            pl.BlockSpec((1, tn), lambda j, i: (0, n // tn + j)),
        ],
        out_specs=pl.BlockSpec((tm, tn), lambda j, i: (i, j)),
        out_shape=jax.ShapeDtypeStruct((t, n), BF16),
        scratch_shapes=[pltpu.VMEM((d, tn), BF16), pltpu.VMEM((d, tn), BF16)],
        compiler_params=_cparams("parallel", "arbitrary"),
        name="merge",
    )(ya, yb, wa, wb, proj, proj, gate_bias, gate_bias)


def _matmul_resid_kernel(a_ref, w_ref, r_ref, o_ref, ws_ref):
    @pl.when(pl.program_id(1) == 0)
    def _():
        _cast_weight(w_ref, ws_ref)

    o_ref[...] = r_ref[...] + _dot(a_ref[...], ws_ref[...])


def _matmul_resid(a, w, layer, resid, *, tm, tn, name):
    t, k = a.shape
    n = w.shape[2]
    return pl.pallas_call(
        _matmul_resid_kernel,
        grid=(n // tn, t // tm),
        in_specs=[
            pl.BlockSpec((tm, k), lambda j, i: (i, 0)),
            pl.BlockSpec((1, k, tn), lambda j, i: (layer, 0, j)),
            pl.BlockSpec((tm, tn), lambda j, i: (i, j)),
        ],
        out_specs=pl.BlockSpec((tm, tn), lambda j, i: (i, j)),
        out_shape=jax.ShapeDtypeStruct((t, n), F32),
        scratch_shapes=[pltpu.VMEM((k, tn), BF16)],
        compiler_params=_cparams("parallel", "arbitrary"),
        name=name,
    )(a, w, resid)


def _ffn_gu_kernel(x_ref, g_ref, wg_ref, wu_ref, o_ref, h_ref):
    @pl.when(pl.program_id(1) == 0)
    def _():
        h_ref[...] = _rmsnorm_rows(x_ref[...], g_ref[...]).astype(BF16)

    h = h_ref[...]
    gate = _dot(h, wg_ref[0].astype(BF16))
    up = _dot(h, wu_ref[0].astype(BF16))
    o_ref[...] = (_silu(gate) * up).astype(o_ref.dtype)


def _ffn_gu(x, g, w_gu, layer, *, tm=1024, tf=512):
    t, d = x.shape
    f = w_gu.shape[2] // 2
    nf = f // tf
    return pl.pallas_call(
        _ffn_gu_kernel,
        grid=(t // tm, nf),
        in_specs=[
            pl.BlockSpec((tm, d), lambda i, j: (i, 0)),
            pl.BlockSpec((1, d), lambda i, j: (0, 0)),
            pl.BlockSpec((1, d, tf), lambda i, j: (layer, 0, j)),
            pl.BlockSpec((1, d, tf), lambda i, j: (layer, 0, nf + j)),
        ],
        out_specs=pl.BlockSpec((tm, tf), lambda i, j: (i, j)),
        out_shape=jax.ShapeDtypeStruct((t, f), BF16),
        scratch_shapes=[pltpu.VMEM((tm, d), BF16)],
        compiler_params=_cparams("parallel", "arbitrary"),
        name="ffn_gu",
    )(x, g, w_gu, w_gu)


def _pack_pairs(lo, hi):
    lo_bits = lax.bitcast_convert_type(lo.astype(BF16).astype(F32), jnp.uint32)
    hi_bits = lax.bitcast_convert_type(hi.astype(BF16).astype(F32), jnp.uint32)
    return (hi_bits & jnp.uint32(0xFFFF0000)) | (lo_bits >> 16)


def _unpack_pairs(packed):
    lo = lax.bitcast_convert_type(packed << 16, F32)
    hi = lax.bitcast_convert_type(packed & jnp.uint32(0xFFFF0000), F32)
    return lo, hi


def _router_kernel(x_ref, g_ref, wr_ref, br_ref, h_ref, r_ref):
    h = _rmsnorm_rows(x_ref[...], g_ref[...])
    half = h.shape[1] // 2
    h_ref[...] = _pack_pairs(h[:, :half], h[:, half:])
    h1, h2, h3 = _split3(h)
    w = wr_ref[...]
    w1, w2, w3 = _split3(w)
    logits = (_dot(h1, w1) + (_dot(h1, w2) + _dot(h2, w1))
              + (_dot(h1, w3) + _dot(h2, w2) + _dot(h3, w1))) + br_ref[...]
    lane = lax.broadcasted_iota(jnp.int32, logits.shape, 1)
    logits = jnp.where(lane < N_EXPERTS, logits, -jnp.inf)
    m1 = jnp.max(logits, axis=-1, keepdims=True)
    i1 = jnp.min(jnp.where(logits == m1, lane, SM_W), axis=-1, keepdims=True)
    rest = jnp.where(lane == i1, -jnp.inf, logits)
    m2 = jnp.max(rest, axis=-1, keepdims=True)
    i2 = jnp.min(jnp.where(rest == m2, lane, SM_W), axis=-1, keepdims=True)
    e = jnp.exp(m2 - m1)
    p1 = 1.0 / (1.0 + e)
    p2 = e / (1.0 + e)
    r_ref[...] = jnp.where(lane == 0, i1.astype(F32),
                           jnp.where(lane == 1, i2.astype(F32),
                                     jnp.where(lane == 2, p1, jnp.where(lane == 3, p2, 0.0))))


def _router(x, g, w_router, b_router, *, tm=512):
    t, d = x.shape
    return pl.pallas_call(
        _router_kernel,
        grid=(t // tm,),
        in_specs=[
            pl.BlockSpec((tm, d), lambda i: (i, 0)),
            pl.BlockSpec((1, d), lambda i: (0, 0)),
            pl.BlockSpec((d, SM_W), lambda i: (0, 0)),
            pl.BlockSpec((1, SM_W), lambda i: (0, 0)),
        ],
        out_specs=[
            pl.BlockSpec((tm, d // 2), lambda i: (i, 0)),
            pl.BlockSpec((tm, SM_W), lambda i: (i, 0)),
        ],
        out_shape=[
            jax.ShapeDtypeStruct((t, d // 2), jnp.uint32),
            jax.ShapeDtypeStruct((t, SM_W), F32),
        ],
        compiler_params=_cparams("parallel"),
        name="router",
    )(x, g, w_router, b_router)


def _row_copy(src_ref, dst_ref, sem, src_row, dst_row):
    return pltpu.make_async_copy(src_ref.at[pl.ds(src_row, 1), :], dst_ref.at[pl.ds(dst_row, 1), :], sem)


def _gather_kernel(idx_ref, src_ref, o_ref, buf_ref, sem):
    rows = buf_ref.shape[1]
    step = pl.program_id(0)
    slot = step % 2

    def issue(for_step, into):
        base = for_step * rows

        def start(blk, carry):
            for u in range(DMA_UNROLL):
                r = blk * DMA_UNROLL + u
                _row_copy(src_ref, buf_ref.at[into], sem.at[into], idx_ref[base + r], r).start(priority=u % 2)
            return carry

        lax.fori_loop(0, rows // DMA_UNROLL, start, 0)

    @pl.when(step == 0)
    def _():
        issue(step, slot)

    @pl.when(step + 1 < pl.num_programs(0))
    def _():
        issue(step + 1, 1 - slot)

    def wait(blk, carry):
        for u in range(DMA_UNROLL):
            _row_copy(src_ref, buf_ref.at[slot], sem.at[slot], 0, blk * DMA_UNROLL + u).wait()
        return carry

    lax.fori_loop(0, rows // DMA_UNROLL, wait, 0)
    lo, hi = _unpack_pairs(buf_ref[slot])
    half = buf_ref.shape[2]
    o_ref[:, :half] = lo.astype(o_ref.dtype)
    o_ref[:, half:] = hi.astype(o_ref.dtype)


def _gather_rows(src, idx, *, rows=512):
    p = idx.shape[0]
    half = src.shape[1]
    d = 2 * half
    return pl.pallas_call(
        _gather_kernel,
        grid_spec=pltpu.PrefetchScalarGridSpec(
            num_scalar_prefetch=1,
            grid=(p // rows,),
            in_specs=[pl.BlockSpec(memory_space=pl.ANY)],
            out_specs=pl.BlockSpec((rows, d), lambda i, idx_ref: (i, 0)),
            scratch_shapes=[pltpu.VMEM((2, rows, half), src.dtype), pltpu.SemaphoreType.DMA((2,))],
        ),
        out_shape=jax.ShapeDtypeStruct((p, d), BF16),
        compiler_params=_cparams("arbitrary"),
        name="moe_gather",
    )(idx, src)


def _expert_changed(te_ref):
    i = pl.program_id(1)
    return jnp.logical_or(i == 0, te_ref[i] != te_ref[jnp.maximum(i - 1, 0)])


def _moe_gu_kernel(te_ref, nu_ref, a_ref, wg_ref, wu_ref, o_ref, wgs_ref, wus_ref):
    used = pl.program_id(1) < nu_ref[0]

    @pl.when(_expert_changed(te_ref))
    def _():
        _cast_weight(wg_ref, wgs_ref)
        _cast_weight(wu_ref, wus_ref)

    @pl.when(used)
    def _():
        a = a_ref[...]
        o_ref[...] = (_silu(_dot(a, wgs_ref[...])) * _dot(a, wus_ref[...])).astype(o_ref.dtype)

    @pl.when(jnp.logical_not(used))
    def _():
        o_ref[...] = jnp.zeros_like(o_ref)


def _moe_gu(a, w_gu, layer, tile_expert, n_used, *, tm, tf=1024):
    p, d = a.shape
    f = w_gu.shape[3] // 2
    nf = f // tf
    return pl.pallas_call(
        _moe_gu_kernel,
        grid_spec=pltpu.PrefetchScalarGridSpec(
            num_scalar_prefetch=2,
            grid=(nf, p // tm),
            in_specs=[
                pl.BlockSpec((tm, d), lambda j, i, te, nu: (i, 0)),
                pl.BlockSpec((1, 1, d, tf), lambda j, i, te, nu: (layer, te[i], 0, j)),
                pl.BlockSpec((1, 1, d, tf), lambda j, i, te, nu: (layer, te[i], 0, nf + j)),
            ],
            out_specs=pl.BlockSpec((tm, tf), lambda j, i, te, nu: (i, j)),
            scratch_shapes=[pltpu.VMEM((d, tf), BF16), pltpu.VMEM((d, tf), BF16)],
        ),
        out_shape=jax.ShapeDtypeStruct((p, f), BF16),
        compiler_params=_cparams("parallel", "arbitrary"),
        name="moe_gu",
    )(tile_expert, n_used, a, w_gu, w_gu)


def _moe_down_kernel(te_ref, nu_ref, a_ref, w_ref, o_ref, ws_ref):
    used = pl.program_id(1) < nu_ref[0]

    @pl.when(_expert_changed(te_ref))
    def _():
        _cast_weight(w_ref, ws_ref)

    @pl.when(used)
    def _():
        y = _dot(a_ref[...], ws_ref[...])
        half = y.shape[1] // 2
        o_ref[...] = _pack_pairs(y[:, :half], y[:, half:])

    @pl.when(jnp.logical_not(used))
    def _():
        o_ref[...] = jnp.zeros_like(o_ref)


def _moe_down(a, w_down, layer, tile_expert, n_used, *, tm, tn=MOE_DOWN_TN):
    p, f = a.shape
    n = w_down.shape[3]
    return pl.pallas_call(
        _moe_down_kernel,
        grid_spec=pltpu.PrefetchScalarGridSpec(
            num_scalar_prefetch=2,
            grid=(n // tn, p // tm),
            in_specs=[
                pl.BlockSpec((tm, f), lambda j, i, te, nu: (i, 0)),
                pl.BlockSpec((1, 1, f, tn), lambda j, i, te, nu: (layer, te[i], 0, j)),
            ],
            out_specs=pl.BlockSpec((tm, tn // 2), lambda j, i, te, nu: (i, j)),
            scratch_shapes=[pltpu.VMEM((f, tn), BF16)],
        ),
        out_shape=jax.ShapeDtypeStruct((p, n // 2), jnp.uint32),
        compiler_params=_cparams("parallel", "arbitrary"),
        name="moe_down",
    )(tile_expert, n_used, a, w_down)


def _combine_kernel(p1_ref, p2_ref, y_ref, x_ref, r_ref, g_ref, o_ref, b1_ref, b2_ref, sem, *, final_norm):
    rows = b1_ref.shape[0]
    base = pl.program_id(0) * rows

    def start(blk, carry):
        for u in range(DMA_UNROLL):
            r = blk * DMA_UNROLL + u
            _row_copy(y_ref, b1_ref, sem.at[0], p1_ref[base + r], r).start(priority=0)
            _row_copy(y_ref, b2_ref, sem.at[1], p2_ref[base + r], r).start(priority=1)
        return carry

    lax.fori_loop(0, rows // DMA_UNROLL, start, 0)

    def wait(blk, carry):
        for u in range(DMA_UNROLL):
            r = blk * DMA_UNROLL + u
            _row_copy(y_ref, b1_ref, sem.at[0], 0, r).wait()
            _row_copy(y_ref, b2_ref, sem.at[1], 0, r).wait()
        return carry

    lax.fori_loop(0, rows // DMA_UNROLL, wait, 0)
    route = r_ref[...]
    w1 = route[:, 2:3]
    w2 = route[:, 3:4]
    lo1, hi1 = _unpack_pairs(b1_ref[...])
    lo2, hi2 = _unpack_pairs(b2_ref[...])
    lo = w1 * lo1 + w2 * lo2
    hi = w1 * hi1 + w2 * hi2
    hw = MOE_DOWN_TN // 2
    pieces = []
    for j in range(lo.shape[1] // hw):
        pieces += [lo[:, j * hw:(j + 1) * hw], hi[:, j * hw:(j + 1) * hw]]
    x_new = x_ref[...] + jnp.concatenate(pieces, axis=1)
    o_ref[...] = _rmsnorm_rows(x_new, g_ref[...]) if final_norm else x_new


def _combine(y_sorted, x, route, pos1, pos2, norm_g, *, final_norm, rows=512):
    t, d = x.shape
    half = y_sorted.shape[1]
    return pl.pallas_call(
        functools.partial(_combine_kernel, final_norm=final_norm),
        grid_spec=pltpu.PrefetchScalarGridSpec(
            num_scalar_prefetch=2,
            grid=(t // rows,),
            in_specs=[
                pl.BlockSpec(memory_space=pl.ANY),
                pl.BlockSpec((rows, d), lambda i, p1, p2: (i, 0)),
                pl.BlockSpec((rows, SM_W), lambda i, p1, p2: (i, 0)),
                pl.BlockSpec((1, d), lambda i, p1, p2: (0, 0)),
            ],
            out_specs=pl.BlockSpec((rows, d), lambda i, p1, p2: (i, 0)),
            scratch_shapes=[pltpu.VMEM((rows, half), jnp.uint32), pltpu.VMEM((rows, half), jnp.uint32),
                            pltpu.SemaphoreType.DMA((2,))],
        ),
        out_shape=jax.ShapeDtypeStruct((t, d), F32),
        compiler_params=_cparams("arbitrary"),
        name="moe_combine",
    )(pos1, pos2, y_sorted, x, route, norm_g)


MOE_TM = 512


def _moe_plan(route, tm):
    t = route.shape[0]
    n_tiles = (t * TOP_K) // tm + N_EXPERTS
    eid = route[:, :TOP_K].astype(jnp.int32).reshape(-1)
    onehot = (eid[:, None] == jnp.arange(N_EXPERTS, dtype=jnp.int32)[None, :]).astype(jnp.int32)
    csum = jnp.cumsum(onehot, axis=0)
    rank = jnp.sum((csum - onehot) * onehot, axis=1)
    counts = csum[-1]
    tiles_per = (counts + tm - 1) // tm
    tile_end = jnp.cumsum(tiles_per)
    start = (tile_end - tiles_per) * tm
    pos = (jnp.sum(onehot * start[None, :], axis=1) + rank).astype(jnp.int32)
    n_used = tile_end[-1].astype(jnp.int32)
    tile_id = jnp.arange(n_tiles, dtype=jnp.int32)
    tile_expert = jnp.sum((tile_id[:, None] >= tile_end[None, :]).astype(jnp.int32), axis=1)
    tile_expert = jnp.minimum(tile_expert, N_EXPERTS - 1).astype(jnp.int32)
    last_expert = jnp.max(jnp.where(counts > 0, jnp.arange(N_EXPERTS, dtype=jnp.int32), 0))
    tile_expert = jnp.where(tile_id < n_used, tile_expert, last_expert).astype(jnp.int32)
    token = jnp.repeat(jnp.arange(t, dtype=jnp.int32), TOP_K)
    row_token = jnp.zeros((n_tiles * tm,), jnp.int32).at[pos].set(token)
    pos = pos.reshape(t, TOP_K)
    return row_token, tile_expert, n_used.reshape(1), pos[:, 0], pos[:, 1]


def _moe_ffn(x, g, w_router, b_router, w_gu, w_down, layer, final_g, final_norm):
    h, route = _router(x, g, w_router, b_router)
    row_token, tile_expert, n_used, pos1, pos2 = _moe_plan(route, MOE_TM)
    a = _gather_rows(h, row_token)
    act = _moe_gu(a, w_gu, layer, tile_expert, n_used, tm=MOE_TM)
    y = _moe_down(act, w_down, layer, tile_expert, n_used, tm=MOE_TM)
    return _combine(y, x, route, pos1, pos2, final_g, final_norm=final_norm)


def _final_norm_kernel(x_ref, g_ref, o_ref):
    o_ref[...] = _rmsnorm_rows(x_ref[...], g_ref[...])


def _final_norm(x, g, *, tm=512):
    t, d = x.shape
    return pl.pallas_call(
        _final_norm_kernel,
        grid=(t // tm,),
        in_specs=[pl.BlockSpec((tm, d), lambda i: (i, 0)), pl.BlockSpec((1, d), lambda i: (0, 0))],
        out_specs=pl.BlockSpec((tm, d), lambda i: (i, 0)),
        out_shape=jax.ShapeDtypeStruct((t, d), F32),
        compiler_params=_cparams("parallel"),
        name="final_norm",
    )(x, g)


def _lane_row(vec, offset):
    return jnp.zeros((1, SM_W), F32).at[0, offset:offset + vec.shape[0]].set(vec.astype(F32))


def _mixer(x, l, bsz, seq, p, consts):
    tril, triu, expand = consts
    h, small, small_t = _norm_small(x, p["norm_mix_g"][l][None, :], p["w_in_t"], l)
    conv_w = jnp.zeros((CONV_K, PC_TOTAL), F32)
    conv_w = conv_w.at[:, PC_Q:PC_V].set(p["qk_conv_w"][l]).at[:, PC_XBC:PC_GATE].set(p["ssd_conv_w"][l])
    conv_b = jnp.zeros((1, PC_TOTAL), F32)
    conv_b = conv_b.at[0, PC_Q:PC_V].set(p["qk_conv_b"][l]).at[0, PC_XBC:PC_GATE].set(p["ssd_conv_b"][l])
    proj = _proj(h, p["w_in_t"], l, conv_w, conv_b, seq=seq)

    if_bias = p["if_bias"][l]
    y_a = _mlstm(proj, small, small_t,
                 _lane_row(if_bias, 0),
                 jnp.broadcast_to(if_bias.astype(F32)[:, None], (2 * ML_HEADS, SEQ_CHUNK)),
                 p["mh_norm_g"][l][None, :], tril, triu, bsz=bsz, seq=seq)
    y_b = _ssd(proj, small, small_t,
               _lane_row(p["dt_bias"][l], SM_DT),
               jnp.broadcast_to(p["dt_bias"][l].astype(F32)[:, None], (SSD_HEADS, SEQ_CHUNK)),
               _lane_row(p["a_log"][l], SM_DT),
               jnp.broadcast_to(p["a_log"][l].astype(F32)[:, None], (SSD_HEADS, SEQ_CHUNK)),
               jnp.repeat(p["d_skip"][l].astype(F32), SSD_P)[None, :],
               p["ssd_norm_g"][l][None, :], tril, triu, expand, bsz=bsz, seq=seq)
    merged = _merge(y_a, y_b, p["w_branch_a"], p["w_branch_b"], l, proj, p["gate_bias"][l][None, :])
    return _matmul_resid(merged, p["w_out"], l, x, tm=1024, tn=1024, name="out_proj")


def kernel(x, norm_mix_g, w_in, if_bias, qk_conv_w, qk_conv_b, mh_norm_g, ssd_conv_w, ssd_conv_b,
           dt_bias, a_log, d_skip, ssd_norm_g, gate_bias, w_branch_a, w_branch_b, w_out, norm_ffn_g,
           ffn_w_gu, ffn_w_down, router_w, router_b, exp_w_gu, exp_w_down, norm_final_g):
    p = dict(norm_mix_g=norm_mix_g, w_in_t=jnp.swapaxes(w_in, 1, 2), if_bias=if_bias, qk_conv_w=qk_conv_w,
             qk_conv_b=qk_conv_b, mh_norm_g=mh_norm_g, ssd_conv_w=ssd_conv_w, ssd_conv_b=ssd_conv_b,
             dt_bias=dt_bias, a_log=a_log, d_skip=d_skip, ssd_norm_g=ssd_norm_g, gate_bias=gate_bias,
             w_branch_a=w_branch_a, w_branch_b=w_branch_b, w_out=w_out)
    bsz, seq, d = x.shape
    assert d == D_MODEL and seq % SEQ_CHUNK == 0 and (bsz * seq) % 2048 == 0, x.shape
    x = x.reshape(bsz * seq, d)

    idx = jnp.arange(SEQ_CHUNK, dtype=jnp.int32)
    tril = (idx[:, None] >= idx[None, :]).astype(BF16)
    triu = (idx[:, None] <= idx[None, :]).astype(BF16)
    lane = jnp.arange(SM_W, dtype=jnp.int32)[:, None]
    chan = jnp.arange(SSD_WIDTH, dtype=jnp.int32)[None, :]
    expand = (lane == SM_DT + chan // SSD_P).astype(BF16)
    consts = (tril, triu, expand)

    for l in range(DEPTH):
        x = _mixer(x, l, bsz, seq, p, consts)
        g = norm_ffn_g[l][None, :]
        if l % 2 == 0:
            act = _ffn_gu(x, g, ffn_w_gu, l // 2)
            x = _matmul_resid(act, ffn_w_down, l // 2, x, tm=512, tn=512, name="ffn_down")
        else:
            w_r = jnp.pad(router_w[l // 2], ((0, 0), (0, SM_W - N_EXPERTS)))
            b_r = _lane_row(router_b[l // 2], 0)
            x = _moe_ffn(x, g, w_r, b_r, exp_w_gu, exp_w_down, l // 2, norm_final_g[None, :], l == DEPTH - 1)
    if DEPTH % 2 == 1:
        x = _final_norm(x, norm_final_g[None, :])
    return x.reshape(bsz, seq, d)
```
